```python
import jax, jax.numpy as jnp
from jax import lax
import numpy as np

D_MODEL = 1024
BATCH = 4
SEQ = 8192
DEPTH = 2

CHUNK = 64
Q_BLOCK = 128
MIX = D_MODEL
W_LRU = MIX // 2
LRU_BLOCKS = 8
LRU_BW = W_LRU // LRU_BLOCKS
LRU_C = 8.0
CONV_A = 4
SB_HEADS = 4
SB_DH = MIX // 4 // SB_HEADS
SB_W = SB_HEADS * SB_DH
DF_HEADS = 4
DF_DV = MIX // 4 // DF_HEADS
DF_DQK = DF_DV // 2
DF_W = DF_HEADS * DF_DV
D_FF = ((8 * D_MODEL // 3 + 255) // 256) * 256
CONV_FF = 3
EPS = 1e-6
IN_SIZES = [W_LRU, W_LRU, SB_W, SB_W, SB_W, DF_W, DF_W, DF_W]
P_IN = sum(IN_SIZES)

kernel_name = 'hybrid_rglru_stickbreak_diffattn_convffn'


def rmsnorm(x, g):
    xf = x.astype(jnp.float32)
    y = xf * lax.rsqrt(jnp.mean(xf * xf, axis=-1, keepdims=True) + EPS)
    return (y * g.astype(jnp.float32)).astype(x.dtype)


def causal_dwconv(x, w, b):
    k = w.shape[0]
    y = lax.conv_general_dilated(
        x, w[:, None, :], window_strides=(1,), padding=[(k - 1, 0)],
        dimension_numbers=('NWC', 'WIO', 'NWC'), feature_group_count=x.shape[-1])
    return y + b


def rg_lru(xa, w_r, b_r, w_i, b_i, lam):
    bn, s, c = xa.shape
    xb = xa.reshape(bn, s, LRU_BLOCKS, LRU_BW)
    r = jax.nn.sigmoid((jnp.einsum('bsnc,ncd->bsnd', xb, w_r).reshape(bn, s, c) + b_r).astype(jnp.float32))
    i = jax.nn.sigmoid((jnp.einsum('bsnc,ncd->bsnd', xb, w_i).reshape(bn, s, c) + b_i).astype(jnp.float32))
    log_a = -LRU_C * r * jax.nn.softplus(-lam.astype(jnp.float32))
    a = jnp.exp(log_a)
    u = jnp.sqrt(-jnp.expm1(2.0 * log_a)) * (i * xa.astype(jnp.float32))

    def combine(left, right):
        a1, b1 = left
        a2, b2 = right
        return a1 * a2, a2 * b1 + b2

    _, h = lax.associative_scan(combine, (a, u), axis=1)
    return h.astype(xa.dtype)


def to_heads(t, n_heads):
    bn, s, w = t.shape
    return t.reshape(bn, s, n_heads, w // n_heads).transpose(0, 2, 1, 3)


def from_blocks(o):
    nb, bn, h, qb, dh = o.shape
    return o.transpose(1, 0, 3, 2, 4).reshape(bn, nb * qb, h, dh)


def stick_breaking(q, k, v):
    s_len = q.shape[2]
    scale = SB_DH ** -0.5
    spos = jnp.arange(s_len)

    def block(start):
        qb = lax.dynamic_slice_in_dim(q, start, Q_BLOCK, axis=2)
        tpos = start + jnp.arange(Q_BLOCK)
        earlier = spos[None, :] < tpos[:, None]
        z = jnp.einsum('bhqd,bhkd->bhqk', qb, k).astype(jnp.float32) * scale
        log_1m = jnp.where(earlier, jax.nn.log_sigmoid(-z), 0.0)
        tail = lax.cumsum(log_1m, axis=3, reverse=True) - log_1m
        wts = jnp.where(earlier, jnp.exp(jax.nn.log_sigmoid(z) + tail), 0.0)
        return jnp.einsum('bhqk,bhkd->bhqd', wts.astype(v.dtype), v)

    starts = jnp.arange(s_len // Q_BLOCK, dtype=jnp.int32) * Q_BLOCK
    o = from_blocks(lax.map(block, starts))
    return o.reshape(o.shape[0], s_len, SB_W)


def diff_attention(q, k, v, lam, lam_init, g_sub):
    s_len = q.shape[2]
    scale = DF_DQK ** -0.5
    q1, q2 = q[..., :DF_DQK], q[..., DF_DQK:]
    k1, k2 = k[..., :DF_DQK], k[..., DF_DQK:]
    slopes = 2.0 ** (-8.0 * (jnp.arange(DF_HEADS, dtype=jnp.float32) + 1.0) / DF_HEADS)
    spos = jnp.arange(s_len)

    def block(start):
        qb1 = lax.dynamic_slice_in_dim(q1, start, Q_BLOCK, axis=2)
        qb2 = lax.dynamic_slice_in_dim(q2, start, Q_BLOCK, axis=2)
        tpos = start + jnp.arange(Q_BLOCK)
        allowed = (spos[None, :] // CHUNK) <= (tpos[:, None] // CHUNK)
        dist = jnp.abs(tpos[:, None] - spos[None, :]).astype(jnp.float32)
        bias = -slopes[:, None, None] * dist

        def probs(qb, kk):
            sc = jnp.einsum('bhqd,bhkd->bhqk', qb, kk).astype(jnp.float32) * scale + bias
            return jax.nn.softmax(jnp.where(allowed, sc, -jnp.inf), axis=-1)

        p = probs(qb1, k1) - lam * probs(qb2, k2)
        return jnp.einsum('bhqk,bhkd->bhqd', p.astype(v.dtype), v)

    starts = jnp.arange(s_len // Q_BLOCK, dtype=jnp.int32) * Q_BLOCK
    o = from_blocks(lax.map(block, starts))
    o = rmsnorm(o, g_sub) * (1.0 - lam_init)
    return o.reshape(o.shape[0], s_len, DF_W)


def setup_inputs(seed: int = 0) -> dict:
    key = jax.random.key(seed)
    ks = jax.random.split(key, 24)
    f32 = jnp.float32

    def nrm(k, shape, fan_in):
        return jax.random.normal(k, shape, f32) * (fan_in ** -0.5)

    def gain(k, shape):
        return 1.0 + 0.02 * jax.random.normal(k, shape, f32)

    def small(k, shape):
        return 0.01 * jax.random.normal(k, shape, f32)

    a_c = jax.random.uniform(ks[8], (DEPTH, W_LRU), f32, minval=0.9, maxval=0.999)
    a_base = a_c ** (1.0 / LRU_C)
    return {
        'x': jax.random.normal(ks[0], (BATCH, SEQ, D_MODEL), f32),
        'norm1_g': gain(ks[1], (DEPTH, D_MODEL)),
        'w_in': nrm(ks[2], (DEPTH, D_MODEL, P_IN), D_MODEL),
        'conv_a_w': nrm(ks[3], (DEPTH, CONV_A, W_LRU), CONV_A),
        'conv_a_b': small(ks[4], (DEPTH, W_LRU)),
        'w_rgate': nrm(ks[5], (DEPTH, LRU_BLOCKS, LRU_BW, LRU_BW), LRU_BW),
        'b_rgate': small(ks[6], (DEPTH, W_LRU)),
        'w_igate': nrm(ks[7], (DEPTH, LRU_BLOCKS, LRU_BW, LRU_BW), LRU_BW),
        'b_igate': small(ks[9], (DEPTH, W_LRU)),
        'lru_lambda': jnp.log(a_base) - jnp.log1p(-a_base),
        'lam_q1': 0.1 * jax.random.normal(ks[10], (DEPTH, DF_DQK), f32),
        'lam_k1': 0.1 * jax.random.normal(ks[11], (DEPTH, DF_DQK), f32),
        'lam_q2': 0.1 * jax.random.normal(ks[12], (DEPTH, DF_DQK), f32),
        'lam_k2': 0.1 * jax.random.normal(ks[13], (DEPTH, DF_DQK), f32),
        'subln_g': gain(ks[14], (DEPTH, DF_DV)),
        'w_out': nrm(ks[15], (DEPTH, MIX, D_MODEL), MIX),
        'norm2_g': gain(ks[16], (DEPTH, D_MODEL)),
        'w_ff_up': nrm(ks[17], (DEPTH, D_MODEL, 2 * D_FF), D_MODEL),
        'conv_ff_w': nrm(ks[18], (DEPTH, CONV_FF, 2 * D_FF), CONV_FF),
        'conv_ff_b': small(ks[19], (DEPTH, 2 * D_FF)),
        'w_ff_down': nrm(ks[20], (DEPTH, D_FF, D_MODEL), D_FF),
        'final_g': gain(ks[21], (D_MODEL,)),
    }


def reference(x, norm1_g, w_in, conv_a_w, conv_a_b, w_rgate, b_rgate, w_igate, b_igate,
              lru_lambda, lam_q1, lam_k1, lam_q2, lam_k2, subln_g, w_out, norm2_g,
              w_ff_up, conv_ff_w, conv_ff_b, w_ff_down, final_g):
    split_at = np.cumsum(IN_SIZES)[:-1].tolist()
    for l in range(DEPTH):
        h = rmsnorm(x, norm1_g[l])
        proj = h @ w_in[l]
        xa, ya, sq, sk, sv, dq, dk, dv = jnp.split(proj, split_at, axis=-1)

        xa = causal_dwconv(xa, conv_a_w[l], conv_a_b[l])
        out_a = rg_lru(xa, w_rgate[l], b_rgate[l], w_igate[l], b_igate[l], lru_lambda[l]) * jax.nn.gelu(ya)

        out_b = stick_breaking(to_heads(sq, SB_HEADS), to_heads(sk, SB_HEADS), to_heads(sv, SB_HEADS))

        lam_init = 0.8 - 0.6 * float(np.exp(-0.3 * l))
        lam = (jnp.exp(jnp.sum(lam_q1[l] * lam_k1[l]).astype(jnp.float32))
               - jnp.exp(jnp.sum(lam_q2[l] * lam_k2[l]).astype(jnp.float32)) + lam_init)
        out_c = diff_attention(to_heads(dq, DF_HEADS), to_heads(dk, DF_HEADS), to_heads(dv, DF_HEADS),
                               lam, lam_init, subln_g[l])

        mixed = jnp.concatenate([out_a, out_b.astype(out_a.dtype), out_c.astype(out_a.dtype)], axis=-1)
        x = x + mixed @ w_out[l]

        h2 = rmsnorm(x, norm2_g[l])
        u = causal_dwconv(h2 @ w_ff_up[l], conv_ff_w[l], conv_ff_b[l])
        gate, val = jnp.split(u, 2, axis=-1)
        x = x + (jax.nn.silu(gate) * val) @ w_ff_down[l]
    return rmsnorm(x, final_g)
```

```python
import functools

import jax
import jax.numpy as jnp
import numpy as np
from jax import lax
from jax.experimental import pallas as pl
from jax.experimental.pallas import tpu as pltpu

D_MODEL = 1024
BATCH = 4
SEQ = 8192
DEPTH = 2
CHUNK = 64
MIX = D_MODEL
W_LRU = MIX // 2
LRU_BLOCKS = 8
LRU_BW = W_LRU // LRU_BLOCKS
LRU_C = 8.0
CONV_A = 4
SB_HEADS = 4
SB_DH = MIX // 4 // SB_HEADS
SB_W = SB_HEADS * SB_DH
DF_HEADS = 4
DF_DV = MIX // 4 // DF_HEADS
DF_DQK = DF_DV // 2
DF_W = DF_HEADS * DF_DV
D_FF = ((8 * D_MODEL // 3 + 255) // 256) * 256
CONV_FF = 3
EPS = 1e-6
P_IN = 2 * W_LRU + 3 * SB_W + 3 * DF_W

SUBLANES = 8
VMEM_LIMIT = 56 * 1024 * 1024

ATT_BLK = 256
SEG = ATT_BLK // SUBLANES
NEG_BIG = -1e30

F32 = jnp.float32
BF16 = jnp.bfloat16


def _params(sem, vmem=VMEM_LIMIT):
    return pltpu.CompilerParams(dimension_semantics=sem, vmem_limit_bytes=vmem)


def _resident(shape):
    nd = len(shape)
    return pl.BlockSpec(shape, lambda *_: (0,) * nd, pipeline_mode=pl.Buffered(1))


def _rms(x, g):
    return x * lax.rsqrt(jnp.mean(x * x, axis=-1, keepdims=True) + EPS) * g


def _inproj_kernel(x_ref, g_ref, w_ref, xy_ref, sb_ref, df_ref):
    h = _rms(x_ref[...], g_ref[...]).astype(BF16)
    xy_ref[...] = jnp.dot(h, w_ref[:, : 2 * W_LRU], preferred_element_type=F32)
    o = 2 * W_LRU
    sb = jnp.dot(h, w_ref[:, o:o + 3 * SB_W], preferred_element_type=F32)
    sb_ref[:, :SB_W] = (sb[:, :SB_W] * (SB_DH ** -0.5)).astype(BF16)
    sb_ref[:, SB_W:] = sb[:, SB_W:].astype(BF16)
    o += 3 * SB_W
    df = jnp.dot(h, w_ref[:, o:o + 3 * DF_W], preferred_element_type=F32)
    df_ref[:, :DF_W] = (df[:, :DF_W] * (DF_DQK ** -0.5)).astype(BF16)
    df_ref[:, DF_W:] = df[:, DF_W:].astype(BF16)


def _inproj(x2, g, w_bf, tm=512):
    t = x2.shape[0]
    return pl.pallas_call(
        _inproj_kernel,
        grid=(t // tm,),
        in_specs=[
            pl.BlockSpec((tm, D_MODEL), lambda i: (i, 0)),
            _resident((1, D_MODEL)),
            _resident((D_MODEL, P_IN)),
        ],
        out_specs=[
            pl.BlockSpec((tm, 2 * W_LRU), lambda i: (i, 0)),
            pl.BlockSpec((tm, 3 * SB_W), lambda i: (i, 0)),
            pl.BlockSpec((tm, 3 * DF_W), lambda i: (i, 0)),
        ],
        out_shape=[
            jax.ShapeDtypeStruct((t, 2 * W_LRU), F32),
            jax.ShapeDtypeStruct((t, 3 * SB_W), BF16),
            jax.ShapeDtypeStruct((t, 3 * DF_W), BF16),
        ],
        compiler_params=_params(("parallel",)),
        name="inproj",
    )(x2, g, w_bf)


def _softplus(x):
    return jnp.maximum(x, 0.0) + jnp.log(1.0 + jnp.exp(-jnp.abs(x)))


def _sigmoid(x):
    return 1.0 / (1.0 + jnp.exp(-x))


def _lru_kernel(xa_ref, ya_ref, cw_ref, cb_ref, wg_ref, bg_ref, lam_ref, o_ref,
                xbuf, hcar, *, ts):
    s = pl.program_id(1)

    @pl.when(s == 0)
    def _():
        xbuf[0:SUBLANES, :] = jnp.zeros((SUBLANES, W_LRU), F32)
        hcar[...] = jnp.zeros_like(hcar)

    xbuf[SUBLANES:, :] = xa_ref[...]
    xc = cb_ref[...] + cw_ref[CONV_A - 1:CONV_A, :] * xa_ref[...]
    for k in range(1, CONV_A):
        xc = xc + cw_ref[CONV_A - 1 - k:CONV_A - k, :] * xbuf[pl.ds(SUBLANES - k, ts), :]
    xbuf[0:SUBLANES, :] = xa_ref[ts - SUBLANES:, :]

    gates = jnp.dot(xc.astype(BF16), wg_ref[...], preferred_element_type=F32) + bg_ref[...]
    r = _sigmoid(gates[:, :W_LRU])
    ig = _sigmoid(gates[:, W_LRU:])
    log_a = (-LRU_C) * r * _softplus(-lam_ref[...])
    a = jnp.exp(log_a)
    u = jnp.sqrt(1.0 - jnp.exp(2.0 * log_a)) * (ig * xc)

    row = lax.broadcasted_iota(jnp.int32, (ts, W_LRU), 0)
    d = 1
    while d < ts:
        a_sh = jnp.where(row >= d, pltpu.roll(a, d, axis=0), 1.0)
        u_sh = jnp.where(row >= d, pltpu.roll(u, d, axis=0), 0.0)
        u = a * u_sh + u
        a = a * a_sh
        d *= 2
    h = a * hcar[...] + u
    hcar[...] = h[ts - 1:ts, :]
    o_ref[...] = (h * jax.nn.gelu(ya_ref[...])).astype(BF16)


def _lru(xy, cw, cb, wg_bf, bg, lam, ts=256):
    b, s, _ = xy.shape
    return pl.pallas_call(
        functools.partial(_lru_kernel, ts=ts),
        grid=(b, s // ts),
        in_specs=[
            pl.BlockSpec((None, ts, W_LRU), lambda i, j: (i, j, 0)),
            pl.BlockSpec((None, ts, W_LRU), lambda i, j: (i, j, 1)),
            _resident((CONV_A, W_LRU)),
            _resident((1, W_LRU)),
            _resident((W_LRU, 2 * W_LRU)),
            _resident((1, 2 * W_LRU)),
            _resident((1, W_LRU)),
        ],
        out_specs=pl.BlockSpec((None, ts, W_LRU), lambda i, j: (i, j, 0)),
        out_shape=jax.ShapeDtypeStruct((b, s, W_LRU), BF16),
        scratch_shapes=[
            pltpu.VMEM((SUBLANES + ts, W_LRU), F32),
            pltpu.VMEM((1, W_LRU), F32),
        ],
        compiler_params=_params(("parallel", "arbitrary")),
        name="rglru",
    )(xy, xy, cw, cb, wg_bf, bg, lam)


def _perm_local_key():
    row = lax.broadcasted_iota(jnp.int32, (ATT_BLK, ATT_BLK), 0)
    return (row & (SUBLANES - 1)) * SEG + (row >> 3)


def _sb_block(z, run, diag):
    sp = _softplus(z)
    if diag:
        earlier = _perm_local_key() < lax.broadcasted_iota(jnp.int32, z.shape, 1)
        sp = jnp.where(earlier, sp, 0.0)
    lz = z - sp
    sp_r = [sp[r * SUBLANES:(r + 1) * SUBLANES, :] for r in range(SEG)]
    seg_tot = sp_r[0]
    for r in range(1, SEG):
        seg_tot = seg_tot + sp_r[r]
    sub = lax.broadcasted_iota(jnp.int32, seg_tot.shape, 0)
    tail = jnp.broadcast_to(run, seg_tot.shape)
    for u in range(1, SUBLANES):
        tail = tail + jnp.where(sub < u, seg_tot[u:u + 1, :], 0.0)
    ws = [None] * SEG
    for r in reversed(range(SEG)):
        ws[r] = jnp.exp(lz[r * SUBLANES:(r + 1) * SUBLANES, :] - tail)
        tail = tail + sp_r[r]
    w = jnp.concatenate(ws, axis=0)
    if diag:
        w = jnp.where(earlier, w, 0.0)
    return w.astype(BF16), run + jnp.sum(seg_tot, axis=0, keepdims=True)


def _sb_kernel(qt_ref, k_ref, vt_ref, o_ref):
    i = pl.program_id(2)
    qt = qt_ref[...]

    def step(j, run, acc, diag):
        off = pl.multiple_of(j * ATT_BLK, ATT_BLK)
        z = jnp.dot(k_ref[pl.ds(off, ATT_BLK), :], qt, preferred_element_type=F32)
        w, run = _sb_block(z, run, diag)
        acc = acc + jnp.dot(vt_ref[:, pl.ds(off, ATT_BLK)], w, preferred_element_type=F32)
        return run, acc

    run0 = jnp.zeros((1, ATT_BLK), F32)
    acc0 = jnp.zeros((SB_DH, ATT_BLK), F32)
    run, acc = step(i, run0, acc0, True)

    def body(t, carry):
        return step(i - 1 - t, carry[0], carry[1], False)

    run, acc = lax.fori_loop(0, i, body, (run, acc))
    o_ref[...] = acc.astype(o_ref.dtype)


def _stick_breaking(qt, kp, vt):
    b, h, dh, s = qt.shape
    return pl.pallas_call(
        _sb_kernel,
        grid=(b, h, s // ATT_BLK),
        in_specs=[
            pl.BlockSpec((None, None, dh, ATT_BLK), lambda bi, hi, i: (bi, hi, 0, i)),
            pl.BlockSpec((None, None, s, dh), lambda bi, hi, i: (bi, hi, 0, 0)),
            pl.BlockSpec((None, None, dh, s), lambda bi, hi, i: (bi, hi, 0, 0)),
        ],
        out_specs=pl.BlockSpec((None, None, dh, ATT_BLK), lambda bi, hi, i: (bi, hi, 0, i)),
        out_shape=jax.ShapeDtypeStruct((b, h, dh, s), BF16),
        compiler_params=_params(("parallel", "parallel", "arbitrary")),
        name="stickbreak",
    )(qt, kp, vt)


DF_KW = 128
DF_VR = 80


def _df_kernel(slope_ref, qa_ref, qb_ref, k_ref, vt_ref, lq1, lk1, lq2, lk2, gs_ref, o_ref,
               *, lam_init):
    hh = pl.program_id(1)
    i = pl.program_id(2)
    qa = qa_ref[...]
    qb = qb_ref[...]
    slope = slope_ref[hh]

    def update(s_, m, acc, vj):
        m_new = jnp.maximum(m, jnp.max(s_, axis=0, keepdims=True))
        p = jnp.exp(s_ - m_new).astype(BF16)
        acc = jnp.exp(m - m_new) * acc + jnp.dot(vj, p, preferred_element_type=F32)
        return m_new, acc

    def step(j, carry, diag):
        m1, a1, m2, a2 = carry
        off = pl.multiple_of(j * ATT_BLK, ATT_BLK)
        kj = k_ref[pl.ds(off, ATT_BLK), :]
        vj = vt_ref[:, pl.ds(off, ATT_BLK)]
        s1 = jnp.dot(kj, qa, preferred_element_type=F32)
        s2 = jnp.dot(kj, qb, preferred_element_type=F32)
        if diag:
            kpos = lax.broadcasted_iota(jnp.int32, s1.shape, 0)
            qpos = lax.broadcasted_iota(jnp.int32, s1.shape, 1)
            ahead = jnp.maximum(kpos - qpos, 0).astype(F32)
            allowed = (kpos >> 6) <= (qpos >> 6)
            fix = (2.0 * slope) * ahead
            s1 = jnp.where(allowed, s1 - fix, NEG_BIG)
            s2 = jnp.where(allowed, s2 - fix, NEG_BIG)
        m1, a1 = update(s1, m1, a1, vj)
        m2, a2 = update(s2, m2, a2, vj)
        return m1, a1, m2, a2

    m0 = jnp.full((1, ATT_BLK), NEG_BIG, F32)
    a0 = jnp.zeros((DF_VR, ATT_BLK), F32)
    carry = lax.fori_loop(0, i, lambda j, c: step(j, c, False), (m0, a0, m0, a0))
    _, a1, _, a2 = step(i, carry, True)

    lam = (jnp.exp(jnp.sum(lq1[...] * lk1[...], axis=-1, keepdims=True))
           - jnp.exp(jnp.sum(lq2[...] * lk2[...], axis=-1, keepdims=True)) + lam_init)
    o = a1[:DF_DV, :] / a1[DF_DV:DF_DV + 1, :] - lam * (a2[:DF_DV, :] / a2[DF_DV:DF_DV + 1, :])
    o = o * lax.rsqrt(jnp.mean(o * o, axis=0, keepdims=True) + EPS) * gs_ref[...]
    o_ref[...] = (o * (1.0 - lam_init)).astype(o_ref.dtype)


def _diff_attention(slopes, qa, qb, kall, vt, lq1, lk1, lq2, lk2, gs, lam_init):
    b, h, _, s = qa.shape
    vec = _resident((1, DF_DQK))
    return pl.pallas_call(
        functools.partial(_df_kernel, lam_init=lam_init),
        grid=(b, h, s // ATT_BLK),
        in_specs=[
            pl.BlockSpec(memory_space=pltpu.SMEM),
            pl.BlockSpec((None, None, DF_KW, ATT_BLK), lambda bi, hi, i: (bi, hi, 0, i)),
            pl.BlockSpec((None, None, DF_KW, ATT_BLK), lambda bi, hi, i: (bi, hi, 0, i)),
            pl.BlockSpec((None, None, s, DF_KW), lambda bi, hi, i: (bi, hi, 0, 0)),
            pl.BlockSpec((None, None, DF_VR, s), lambda bi, hi, i: (bi, hi, 0, 0)),
            vec, vec, vec, vec,
            _resident((DF_DV, 1)),
        ],
        out_specs=pl.BlockSpec((None, None, DF_DV, ATT_BLK), lambda bi, hi, i: (bi, hi, 0, i)),
        out_shape=jax.ShapeDtypeStruct((b, h, DF_DV, s), BF16),
        compiler_params=_params(("parallel", "parallel", "arbitrary")),
        name="diffattn",
    )(slopes, qa, qb, kall, vt, lq1, lk1, lq2, lk2, gs)


def _outproj_kernel(x_ref, a_ref, b_ref, c_ref, w_ref, o_ref):
    acc = jnp.dot(a_ref[...], w_ref[:W_LRU, :], preferred_element_type=F32)
    acc = acc + jnp.dot(b_ref[...], w_ref[W_LRU:W_LRU + SB_W, :], preferred_element_type=F32)
    acc = acc + jnp.dot(c_ref[...], w_ref[W_LRU + SB_W:, :], preferred_element_type=F32)
    o_ref[...] = x_ref[...] + acc


def _outproj(x2, oa, ob, oc, w_bf, tm=512):
    t = x2.shape[0]
    row = lambda w: pl.BlockSpec((tm, w), lambda i: (i, 0))
    return pl.pallas_call(
        _outproj_kernel,
        grid=(t // tm,),
        in_specs=[row(D_MODEL), row(W_LRU), row(SB_W), row(DF_W), _resident((MIX, D_MODEL))],
        out_specs=row(D_MODEL),
        out_shape=jax.ShapeDtypeStruct((t, D_MODEL), F32),
        compiler_params=_params(("parallel",)),
        name="outproj",
    )(x2, oa, ob, oc, w_bf)


FF_CHUNK = 256
FF_NCHUNK = D_FF // FF_CHUNK


def _ffn_kernel(x_ref, g_ref, wu_ref, cw_ref, cb_ref, wd_ref, fg_ref, o_ref,
                ubuf, tails, acc_ref, *, tm, final_norm):
    s = pl.program_id(1)

    @pl.when(s == 0)
    def _():
        tails[...] = jnp.zeros_like(tails)

    x = x_ref[...]
    h = _rms(x, g_ref[...]).astype(BF16)

    def conv(c, col):
        u = jnp.dot(h, wu_ref[:, pl.ds(col, FF_CHUNK)], preferred_element_type=F32)
        ubuf[0:SUBLANES, :] = tails[c]
        ubuf[SUBLANES:, :] = u
        tails[c] = u[tm - SUBLANES:, :]
        w = cw_ref[:, pl.ds(col, FF_CHUNK)]
        y = cb_ref[:, pl.ds(col, FF_CHUNK)] + w[CONV_FF - 1:CONV_FF, :] * u
        for k in range(1, CONV_FF):
            y = y + w[CONV_FF - 1 - k:CONV_FF - k, :] * ubuf[pl.ds(SUBLANES - k, tm), :]
        return y

    for c in range(FF_NCHUNK):
        gate = conv(2 * c, c * FF_CHUNK)
        val = conv(2 * c + 1, D_FF + c * FF_CHUNK)
        act = (gate * _sigmoid(gate) * val).astype(BF16)
        part = jnp.dot(act, wd_ref[c * FF_CHUNK:(c + 1) * FF_CHUNK, :], preferred_element_type=F32)
        if c == 0:
            acc_ref[...] = part
        else:
            acc_ref[...] += part
    y = x + acc_ref[...]
    if final_norm:
        y = _rms(y, fg_ref[...])
    o_ref[...] = y


def _ffn(x3, g, wu_bf, cw, cb, wd_bf, fg, final_norm, tm=512):
    b, s, _ = x3.shape
    return pl.pallas_call(
        functools.partial(_ffn_kernel, tm=tm, final_norm=final_norm),
        grid=(b, s // tm),
        in_specs=[
            pl.BlockSpec((None, tm, D_MODEL), lambda i, j: (i, j, 0)),
            _resident((1, D_MODEL)),
            _resident((D_MODEL, 2 * D_FF)),
            _resident((CONV_FF, 2 * D_FF)),
            _resident((1, 2 * D_FF)),
            _resident((D_FF, D_MODEL)),
            _resident((1, D_MODEL)),
        ],
        out_specs=pl.BlockSpec((None, tm, D_MODEL), lambda i, j: (i, j, 0)),
        out_shape=jax.ShapeDtypeStruct((b, s, D_MODEL), F32),
        scratch_shapes=[
            pltpu.VMEM((SUBLANES + tm, FF_CHUNK), F32),
            pltpu.VMEM((2 * FF_NCHUNK, SUBLANES, FF_CHUNK), F32),
            pltpu.VMEM((tm, D_MODEL), F32),
        ],
        compiler_params=_params(("parallel", "arbitrary")),
        name="convffn",
    )(x3, g, wu_bf, cw, cb, wd_bf, fg)


def _permute_keys(t):
    b, s = t.shape[:2]
    rest = t.shape[2:]
    t = t.reshape(b, s // ATT_BLK, SUBLANES, SEG, *rest)
    return jnp.swapaxes(t, 2, 3).reshape(b, s, *rest)


def _block_diag(w):
    n, bw, _ = w.shape
    eye = jnp.eye(n, dtype=w.dtype)
    return jnp.einsum("ncd,nm->ncmd", w, eye).reshape(n * bw, n * bw)


def kernel(x, norm1_g, w_in, conv_a_w, conv_a_b, w_rgate, b_rgate, w_igate, b_igate, lru_lambda, lam_q1, lam_k1, lam_q2, lam_k2, subln_g, w_out, norm2_g, w_ff_up, conv_ff_w, conv_ff_b, w_ff_down, final_g):
    b, s, d = x.shape
    t = b * s
    slopes = (2.0 ** (-8.0 * (jnp.arange(DF_HEADS, dtype=F32) + 1.0) / DF_HEADS)).astype(F32)
    pos = jnp.arange(s, dtype=F32)
    pos_hi = jnp.floor(pos / CHUNK) * CHUNK
    pos_lo = pos - pos_hi
    ones_row = jnp.ones((b, DF_HEADS, 1, s), BF16)

    for l in range(DEPTH):
        lam_init = 0.8 - 0.6 * float(np.exp(-0.3 * l))
        xy, sb, df = _inproj(x.reshape(t, d), norm1_g[l][None, :], w_in[l].astype(BF16))

        wg = jnp.concatenate([_block_diag(w_rgate[l]), _block_diag(w_igate[l])], axis=1).astype(BF16)
        bg = jnp.concatenate([b_rgate[l], b_igate[l]])[None, :]
        out_a = _lru(xy.reshape(b, s, 2 * W_LRU), conv_a_w[l], conv_a_b[l][None, :], wg, bg,
                     lru_lambda[l][None, :])

        sb = sb.reshape(b, s, 3, SB_HEADS, SB_DH)
        sqt = sb[:, :, 0].transpose(0, 2, 3, 1)
        skp = _permute_keys(sb[:, :, 1]).transpose(0, 2, 1, 3)
        svt = _permute_keys(sb[:, :, 2]).transpose(0, 2, 3, 1)
        out_b = _stick_breaking(sqt, skp, svt)
        out_b = out_b.transpose(0, 3, 1, 2).reshape(t, SB_W)

        df = df.reshape(b, s, 3, DF_HEADS, DF_DV)
        dqt = df[:, :, 0].transpose(0, 2, 3, 1)
        zq = jnp.zeros((b, DF_HEADS, DF_DQK, s), BF16)
        pad_q = jnp.zeros((b, DF_HEADS, DF_KW - 2 * DF_DQK - 2, s), BF16)
        two_ones = jnp.ones((b, DF_HEADS, 2, s), BF16)
        qa = jnp.concatenate([dqt[:, :, :DF_DQK], zq, two_ones, pad_q], axis=2)
        qb = jnp.concatenate([zq, dqt[:, :, DF_DQK:], two_ones, pad_q], axis=2)
        dk = df[:, :, 1].transpose(0, 2, 1, 3)
        bias = jnp.stack([pos_hi, pos_lo], axis=-1)[None, None] * slopes[None, :, None, None]
        bias = jnp.broadcast_to(bias, (b, DF_HEADS, s, 2)).astype(BF16)
        pad_k = jnp.zeros((b, DF_HEADS, s, DF_KW - 2 * DF_DQK - 2), BF16)
        kall = jnp.concatenate([dk, bias, pad_k], axis=3)
        dvt = df[:, :, 2].transpose(0, 2, 3, 1)
        pad_v = jnp.zeros((b, DF_HEADS, DF_VR - DF_DV - 1, s), BF16)
        vt = jnp.concatenate([dvt, ones_row, pad_v], axis=2)
        out_c = _diff_attention(slopes, qa, qb, kall, vt, lam_q1[l][None, :], lam_k1[l][None, :],
                                lam_q2[l][None, :], lam_k2[l][None, :], subln_g[l][:, None], lam_init)
        out_c = out_c.transpose(0, 3, 1, 2).reshape(t, DF_W)

        x1 = _outproj(x.reshape(t, d), out_a.reshape(t, W_LRU), out_b, out_c, w_out[l].astype(BF16))
        x = _ffn(x1.reshape(b, s, d), norm2_g[l][None, :], w_ff_up[l].astype(BF16), conv_ff_w[l],
                 conv_ff_b[l][None, :], w_ff_down[l].astype(BF16), final_g[None, :], l == DEPTH - 1)
    return x
```

```python
import functools

import jax
import jax.numpy as jnp
import numpy as np
from jax import lax
from jax.experimental import pallas as pl
from jax.experimental.pallas import tpu as pltpu

D_MODEL = 1024
BATCH = 4
SEQ = 8192
DEPTH = 2
CHUNK = 64
MIX = D_MODEL
W_LRU = MIX // 2
LRU_BLOCKS = 8
LRU_BW = W_LRU // LRU_BLOCKS
LRU_C = 8.0
CONV_A = 4
SB_HEADS = 4
SB_DH = MIX // 4 // SB_HEADS
SB_W = SB_HEADS * SB_DH
DF_HEADS = 4
DF_DV = MIX // 4 // DF_HEADS
DF_DQK = DF_DV // 2
DF_W = DF_HEADS * DF_DV
D_FF = ((8 * D_MODEL // 3 + 255) // 256) * 256
CONV_FF = 3
EPS = 1e-6
P_IN = 2 * W_LRU + 3 * SB_W + 3 * DF_W

SUBLANES = 8
VMEM_LIMIT = 56 * 1024 * 1024

ATT_BLK = 256
SEG = ATT_BLK // SUBLANES
NEG_BIG = -1e30
LOG2E = 1.4426950408889634
SB_EXIT = 150.0

F32 = jnp.float32
BF16 = jnp.bfloat16


def _params(sem, vmem=VMEM_LIMIT):
    return pltpu.CompilerParams(dimension_semantics=sem, vmem_limit_bytes=vmem)


def _resident(shape):
    nd = len(shape)
    return pl.BlockSpec(shape, lambda *_: (0,) * nd, pipeline_mode=pl.Buffered(1))


def _rms(x, g):
    return x * lax.rsqrt(jnp.mean(x * x, axis=-1, keepdims=True) + EPS) * g


def _inproj_kernel(x_ref, g_ref, w_ref, xy_ref, sb_ref, df_ref):
    h = _rms(x_ref[...], g_ref[...]).astype(BF16)
    xy_ref[...] = jnp.dot(h, w_ref[:, : 2 * W_LRU], preferred_element_type=F32)
    o = 2 * W_LRU
    sb = jnp.dot(h, w_ref[:, o:o + 3 * SB_W], preferred_element_type=F32)
    sb_ref[:, :SB_W] = (sb[:, :SB_W] * (SB_DH ** -0.5 * LOG2E)).astype(BF16)
    sb_ref[:, SB_W:] = sb[:, SB_W:].astype(BF16)
    o += 3 * SB_W
    df = jnp.dot(h, w_ref[:, o:o + 3 * DF_W], preferred_element_type=F32)
    df_ref[:, :DF_W] = (df[:, :DF_W] * (DF_DQK ** -0.5 * LOG2E)).astype(BF16)
    df_ref[:, DF_W:] = df[:, DF_W:].astype(BF16)


def _inproj(x2, g, w_bf, tm=512):
    t = x2.shape[0]
    return pl.pallas_call(
        _inproj_kernel,
        grid=(t // tm,),
        in_specs=[
            pl.BlockSpec((tm, D_MODEL), lambda i: (i, 0)),
            _resident((1, D_MODEL)),
            _resident((D_MODEL, P_IN)),
        ],
        out_specs=[
            pl.BlockSpec((tm, 2 * W_LRU), lambda i: (i, 0)),
            pl.BlockSpec((tm, 3 * SB_W), lambda i: (i, 0)),
            pl.BlockSpec((tm, 3 * DF_W), lambda i: (i, 0)),
        ],
        out_shape=[
            jax.ShapeDtypeStruct((t, 2 * W_LRU), F32),
            jax.ShapeDtypeStruct((t, 3 * SB_W), BF16),
            jax.ShapeDtypeStruct((t, 3 * DF_W), BF16),
        ],
        compiler_params=_params(("parallel",)),
        name="inproj",
    )(x2, g, w_bf)


def _softplus(x):
    return jnp.maximum(x, 0.0) + jnp.log(1.0 + jnp.exp(-jnp.abs(x)))


def _sigmoid(x):
    return 1.0 / (1.0 + jnp.exp(-x))


def _lru_kernel(xa_ref, ya_ref, cw_ref, cb_ref, wg_ref, bg_ref, lam_ref, o_ref,
                xbuf, hcar, *, ts):
    s = pl.program_id(1)

    @pl.when(s == 0)
    def _():
        xbuf[0:SUBLANES, :] = jnp.zeros((SUBLANES, W_LRU), F32)
        hcar[...] = jnp.zeros_like(hcar)

    xbuf[SUBLANES:, :] = xa_ref[...]
    xc = cb_ref[...] + cw_ref[CONV_A - 1:CONV_A, :] * xa_ref[...]
    for k in range(1, CONV_A):
        xc = xc + cw_ref[CONV_A - 1 - k:CONV_A - k, :] * xbuf[pl.ds(SUBLANES - k, ts), :]
    xbuf[0:SUBLANES, :] = xa_ref[ts - SUBLANES:, :]

    gates = jnp.dot(xc.astype(BF16), wg_ref[...], preferred_element_type=F32) + bg_ref[...]
    r = _sigmoid(gates[:, :W_LRU])
    ig = _sigmoid(gates[:, W_LRU:])
    log_a = (-LRU_C) * r * _softplus(-lam_ref[...])
    a = jnp.exp(log_a)
    u = jnp.sqrt(1.0 - jnp.exp(2.0 * log_a)) * (ig * xc)

    row = lax.broadcasted_iota(jnp.int32, (ts, W_LRU), 0)
    d = 1
    while d < ts:
        a_sh = jnp.where(row >= d, pltpu.roll(a, d, axis=0), 1.0)
        u_sh = jnp.where(row >= d, pltpu.roll(u, d, axis=0), 0.0)
        u = a * u_sh + u
        a = a * a_sh
        d *= 2
    h = a * hcar[...] + u
    hcar[...] = h[ts - 1:ts, :]
    o_ref[...] = (h * jax.nn.gelu(ya_ref[...])).astype(BF16)


def _lru(xy, cw, cb, wg_bf, bg, lam, ts=256):
    b, s, _ = xy.shape
    return pl.pallas_call(
        functools.partial(_lru_kernel, ts=ts),
        grid=(b, s // ts),
        in_specs=[
            pl.BlockSpec((None, ts, W_LRU), lambda i, j: (i, j, 0)),
            pl.BlockSpec((None, ts, W_LRU), lambda i, j: (i, j, 1)),
            _resident((CONV_A, W_LRU)),
            _resident((1, W_LRU)),
            _resident((W_LRU, 2 * W_LRU)),
            _resident((1, 2 * W_LRU)),
            _resident((1, W_LRU)),
        ],
        out_specs=pl.BlockSpec((None, ts, W_LRU), lambda i, j: (i, j, 0)),
        out_shape=jax.ShapeDtypeStruct((b, s, W_LRU), BF16),
        scratch_shapes=[
            pltpu.VMEM((SUBLANES + ts, W_LRU), F32),
            pltpu.VMEM((1, W_LRU), F32),
        ],
        compiler_params=_params(("parallel", "arbitrary")),
        name="rglru",
    )(xy, xy, cw, cb, wg_bf, bg, lam)


def _perm_local_key():
    row = lax.broadcasted_iota(jnp.int32, (ATT_BLK, ATT_BLK), 0)
    return (row & (SUBLANES - 1)) * SEG + (row >> 3)


def _softplus2(x):
    return jnp.maximum(x, 0.0) + jnp.log2(1.0 + jnp.exp2(-jnp.abs(x)))


def _sb_block(z, run, diag):
    sp = _softplus2(z)
    if diag:
        earlier = _perm_local_key() < lax.broadcasted_iota(jnp.int32, z.shape, 1)
        sp = jnp.where(earlier, sp, 0.0)
    lz = z - sp
    sp_r = [sp[r * SUBLANES:(r + 1) * SUBLANES, :] for r in range(SEG)]
    seg_tot = sp_r[0]
    for r in range(1, SEG):
        seg_tot = seg_tot + sp_r[r]
    sub = lax.broadcasted_iota(jnp.int32, seg_tot.shape, 0)
    tail = jnp.broadcast_to(run, seg_tot.shape)
    for u in range(1, SUBLANES):
        tail = tail + jnp.where(sub < u, seg_tot[u:u + 1, :], 0.0)
    ws = [None] * SEG
    for r in reversed(range(SEG)):
        ws[r] = jnp.exp2(lz[r * SUBLANES:(r + 1) * SUBLANES, :] - tail)
        tail = tail + sp_r[r]
    w = jnp.concatenate(ws, axis=0)
    if diag:
        w = jnp.where(earlier, w, 0.0)
    return w.astype(BF16), run + jnp.sum(seg_tot, axis=0, keepdims=True)


def _sb_kernel(qt_ref, k_ref, vt_ref, o_ref):
    i = pl.program_id(1)

    def step(j, runs, accs, diag):
        off = pl.multiple_of(j * ATT_BLK, ATT_BLK)
        new_runs, new_accs = [], []
        zs = [jnp.dot(k_ref[h, pl.ds(off, ATT_BLK), :], qt_ref[h], preferred_element_type=F32)
              for h in range(SB_HEADS)]
        for h in range(SB_HEADS):
            w, run = _sb_block(zs[h], runs[h], diag)
            new_runs.append(run)
            new_accs.append(accs[h] + jnp.dot(vt_ref[h, :, pl.ds(off, ATT_BLK)], w,
                                              preferred_element_type=F32))
        return tuple(new_runs), tuple(new_accs)

    def least(runs):
        low = runs[0]
        for r in runs[1:]:
            low = jnp.minimum(low, r)
        return jnp.min(low)

    runs = tuple(jnp.zeros((1, ATT_BLK), F32) for _ in range(SB_HEADS))
    accs = tuple(jnp.zeros((SB_DH, ATT_BLK), F32) for _ in range(SB_HEADS))
    runs, accs = step(i, runs, accs, True)

    def cond(c):
        return jnp.logical_and(c[0] >= 0, c[1] <= SB_EXIT)

    def body(c):
        j, _, runs, accs = c
        runs, accs = step(j, runs, accs, False)
        return j - 1, least(runs), runs, accs

    _, _, _, accs = lax.while_loop(cond, body, (i - 1, least(runs), runs, accs))
    for h in range(SB_HEADS):
        o_ref[h] = accs[h].astype(o_ref.dtype)


def _stick_breaking(qt, kp, vt):
    b, h, dh, s = qt.shape
    return pl.pallas_call(
        _sb_kernel,
        grid=(b, s // ATT_BLK),
        in_specs=[
            pl.BlockSpec((None, h, dh, ATT_BLK), lambda bi, i: (bi, 0, 0, i)),
            pl.BlockSpec((None, h, s, dh), lambda bi, i: (bi, 0, 0, 0)),
            pl.BlockSpec((None, h, dh, s), lambda bi, i: (bi, 0, 0, 0)),
        ],
        out_specs=pl.BlockSpec((None, h, dh, ATT_BLK), lambda bi, i: (bi, 0, 0, i)),
        out_shape=jax.ShapeDtypeStruct((b, h, dh, s), BF16),
        compiler_params=_params(("parallel", "arbitrary")),
        name="stickbreak",
    )(qt, kp, vt)


DF_KW = 128
DF_NBIAS = 3
DF_VR = 80
DF_SLOPES = tuple(2.0 ** (-8.0 * (h + 1) / DF_HEADS) for h in range(DF_HEADS))


def _df_kernel(qa_ref, qb_ref, k_ref, vt_ref, lq1, lk1, lq2, lk2, gs_ref, o_ref, *, lam_init):
    i = pl.program_id(1)

    def update(s_, m, acc, vj):
        m_new = jnp.maximum(m, jnp.max(s_, axis=0, keepdims=True))
        p = jnp.exp2(s_ - m_new).astype(BF16)
        acc = jnp.exp2(m - m_new) * acc + jnp.dot(vj, p, preferred_element_type=F32)
        return m_new, acc

    def step(j, carry, diag):
        off = pl.multiple_of(j * ATT_BLK, ATT_BLK)
        if diag:
            kpos = lax.broadcasted_iota(jnp.int32, (ATT_BLK, ATT_BLK), 0)
            qpos = lax.broadcasted_iota(jnp.int32, (ATT_BLK, ATT_BLK), 1)
            ahead = jnp.maximum(kpos - qpos, 0).astype(F32)
            allowed = (kpos >> 6) <= (qpos >> 6)
        out = []
        scores = []
        for h in range(DF_HEADS):
            kj = k_ref[h, pl.ds(off, ATT_BLK), :]
            scores.append((jnp.dot(kj, qa_ref[h], preferred_element_type=F32),
                           jnp.dot(kj, qb_ref[h], preferred_element_type=F32)))
        for h in range(DF_HEADS):
            m1, a1, m2, a2 = carry[h]
            vj = vt_ref[h, :, pl.ds(off, ATT_BLK)]
            s1, s2 = scores[h]
            if diag:
                fix = (2.0 * DF_SLOPES[h] * LOG2E) * ahead
                s1 = jnp.where(allowed, s1 - fix, NEG_BIG)
                s2 = jnp.where(allowed, s2 - fix, NEG_BIG)
            m1, a1 = update(s1, m1, a1, vj)
            m2, a2 = update(s2, m2, a2, vj)
            out.append((m1, a1, m2, a2))
        return tuple(out)

    m0 = jnp.full((1, ATT_BLK), NEG_BIG, F32)
    a0 = jnp.zeros((DF_VR, ATT_BLK), F32)
    carry = tuple((m0, a0, m0, a0) for _ in range(DF_HEADS))
    carry = lax.fori_loop(0, i, lambda j, c: step(j, c, False), carry)
    carry = step(i, carry, True)

    lam = (jnp.exp(jnp.sum(lq1[...] * lk1[...], axis=-1, keepdims=True))
           - jnp.exp(jnp.sum(lq2[...] * lk2[...], axis=-1, keepdims=True)) + lam_init)
    for h in range(DF_HEADS):
        _, a1, _, a2 = carry[h]
        o = a1[:DF_DV, :] / a1[DF_DV:DF_DV + 1, :] - lam * (a2[:DF_DV, :] / a2[DF_DV:DF_DV + 1, :])
        o = o * lax.rsqrt(jnp.mean(o * o, axis=0, keepdims=True) + EPS) * gs_ref[...]
        o_ref[h] = (o * (1.0 - lam_init)).astype(o_ref.dtype)


def _diff_attention(qa, qb, kall, vt, lq1, lk1, lq2, lk2, gs, lam_init):
    b, h, _, s = qa.shape
    vec = _resident((1, DF_DQK))
    return pl.pallas_call(
        functools.partial(_df_kernel, lam_init=lam_init),
        grid=(b, s // ATT_BLK),
        in_specs=[
            pl.BlockSpec((None, h, DF_KW, ATT_BLK), lambda bi, i: (bi, 0, 0, i)),
            pl.BlockSpec((None, h, DF_KW, ATT_BLK), lambda bi, i: (bi, 0, 0, i)),
            pl.BlockSpec((None, h, s, DF_KW), lambda bi, i: (bi, 0, 0, 0)),
            pl.BlockSpec((None, h, DF_VR, s), lambda bi, i: (bi, 0, 0, 0)),
            vec, vec, vec, vec,
            _resident((DF_DV, 1)),
        ],
        out_specs=pl.BlockSpec((None, h, DF_DV, ATT_BLK), lambda bi, i: (bi, 0, 0, i)),
        out_shape=jax.ShapeDtypeStruct((b, h, DF_DV, s), BF16),
        compiler_params=_params(("parallel", "arbitrary")),
        name="diffattn",
    )(qa, qb, kall, vt, lq1, lk1, lq2, lk2, gs)


def _outproj_kernel(x_ref, a_ref, b_ref, c_ref, w_ref, o_ref):
    acc = jnp.dot(a_ref[...], w_ref[:W_LRU, :], preferred_element_type=F32)
    acc = acc + jnp.dot(b_ref[...], w_ref[W_LRU:W_LRU + SB_W, :], preferred_element_type=F32)
    acc = acc + jnp.dot(c_ref[...], w_ref[W_LRU + SB_W:, :], preferred_element_type=F32)
    o_ref[...] = x_ref[...] + acc


def _outproj(x2, oa, ob, oc, w_bf, tm=512):
    t = x2.shape[0]
    row = lambda w: pl.BlockSpec((tm, w), lambda i: (i, 0))
    return pl.pallas_call(
        _outproj_kernel,
        grid=(t // tm,),
        in_specs=[row(D_MODEL), row(W_LRU), row(SB_W), row(DF_W), _resident((MIX, D_MODEL))],
        out_specs=row(D_MODEL),
        out_shape=jax.ShapeDtypeStruct((t, D_MODEL), F32),
        compiler_params=_params(("parallel",)),
        name="outproj",
    )(x2, oa, ob, oc, w_bf)


FF_CHUNK = 256
FF_NCHUNK = D_FF // FF_CHUNK


def _ffn_kernel(x_ref, g_ref, wu_ref, cw_ref, cb_ref, wd_ref, fg_ref, o_ref,
                ubuf, tails, acc_ref, *, tm, final_norm):
    s = pl.program_id(1)

    @pl.when(s == 0)
    def _():
        tails[...] = jnp.zeros_like(tails)

    x = x_ref[...]
    h = _rms(x, g_ref[...]).astype(BF16)

    def conv(c, col):
        u = jnp.dot(h, wu_ref[:, pl.ds(col, FF_CHUNK)], preferred_element_type=F32)
        ubuf[0:SUBLANES, :] = tails[c]
        ubuf[SUBLANES:, :] = u
        tails[c] = u[tm - SUBLANES:, :]
        w = cw_ref[:, pl.ds(col, FF_CHUNK)]
        y = cb_ref[:, pl.ds(col, FF_CHUNK)] + w[CONV_FF - 1:CONV_FF, :] * u
        for k in range(1, CONV_FF):
            y = y + w[CONV_FF - 1 - k:CONV_FF - k, :] * ubuf[pl.ds(SUBLANES - k, tm), :]
        return y

    for c in range(FF_NCHUNK):
        gate = conv(2 * c, c * FF_CHUNK)
        val = conv(2 * c + 1, D_FF + c * FF_CHUNK)
        act = (gate * _sigmoid(gate) * val).astype(BF16)
        part = jnp.dot(act, wd_ref[c * FF_CHUNK:(c + 1) * FF_CHUNK, :], preferred_element_type=F32)
        if c == 0:
            acc_ref[...] = part
        else:
            acc_ref[...] += part
    y = x + acc_ref[...]
    if final_norm:
        y = _rms(y, fg_ref[...])
    o_ref[...] = y


def _ffn(x3, g, wu_bf, cw, cb, wd_bf, fg, final_norm, tm=512):
    b, s, _ = x3.shape
    return pl.pallas_call(
        functools.partial(_ffn_kernel, tm=tm, final_norm=final_norm),
        grid=(b, s // tm),
        in_specs=[
            pl.BlockSpec((None, tm, D_MODEL), lambda i, j: (i, j, 0)),
            _resident((1, D_MODEL)),
            _resident((D_MODEL, 2 * D_FF)),
            _resident((CONV_FF, 2 * D_FF)),
            _resident((1, 2 * D_FF)),
            _resident((D_FF, D_MODEL)),
            _resident((1, D_MODEL)),
        ],
        out_specs=pl.BlockSpec((None, tm, D_MODEL), lambda i, j: (i, j, 0)),
        out_shape=jax.ShapeDtypeStruct((b, s, D_MODEL), F32),
        scratch_shapes=[
            pltpu.VMEM((SUBLANES + tm, FF_CHUNK), F32),
            pltpu.VMEM((2 * FF_NCHUNK, SUBLANES, FF_CHUNK), F32),
            pltpu.VMEM((tm, D_MODEL), F32),
        ],
        compiler_params=_params(("parallel", "arbitrary")),
        name="convffn",
    )(x3, g, wu_bf, cw, cb, wd_bf, fg)


def _permute_keys(t):
    b, s = t.shape[:2]
    rest = t.shape[2:]
    t = t.reshape(b, s // ATT_BLK, SUBLANES, SEG, *rest)
    return jnp.swapaxes(t, 2, 3).reshape(b, s, *rest)


def _alibi_bias_terms(s):
    resid = (np.asarray(DF_SLOPES, np.float32)[:, None] * np.float32(LOG2E)
             * np.arange(s, dtype=np.float32)[None, :]).astype(np.float32)
    terms = []
    for _ in range(DF_NBIAS):
        top = (resid.view(np.uint32) & np.uint32(0xFFFF0000)).view(np.float32)
        terms.append(top)
        resid = resid - top
    assert not resid.any()
    return jnp.asarray(np.stack(terms, axis=-1)).astype(BF16)


def _block_diag(w):
    n, bw, _ = w.shape
    eye = jnp.eye(n, dtype=w.dtype)
    return jnp.einsum("ncd,nm->ncmd", w, eye).reshape(n * bw, n * bw)


def kernel(x, norm1_g, w_in, conv_a_w, conv_a_b, w_rgate, b_rgate, w_igate, b_igate, lru_lambda, lam_q1, lam_k1, lam_q2, lam_k2, subln_g, w_out, norm2_g, w_ff_up, conv_ff_w, conv_ff_b, w_ff_down, final_g):
    b, s, d = x.shape
    t = b * s
    bias = jnp.broadcast_to(_alibi_bias_terms(s)[None], (b, DF_HEADS, s, DF_NBIAS))
    ones_row = jnp.ones((b, DF_HEADS, 1, s), BF16)

    for l in range(DEPTH):
        lam_init = 0.8 - 0.6 * float(np.exp(-0.3 * l))
        xy, sb, df = _inproj(x.reshape(t, d), norm1_g[l][None, :], w_in[l].astype(BF16))

        wg = jnp.concatenate([_block_diag(w_rgate[l]), _block_diag(w_igate[l])], axis=1).astype(BF16)
        bg = jnp.concatenate([b_rgate[l], b_igate[l]])[None, :]
        out_a = _lru(xy.reshape(b, s, 2 * W_LRU), conv_a_w[l], conv_a_b[l][None, :], wg, bg,
                     lru_lambda[l][None, :])

        sb = sb.reshape(b, s, 3, SB_HEADS, SB_DH)
        sqt = sb[:, :, 0].transpose(0, 2, 3, 1)
        skp = _permute_keys(sb[:, :, 1]).transpose(0, 2, 1, 3)
        svt = _permute_keys(sb[:, :, 2]).transpose(0, 2, 3, 1)
        out_b = _stick_breaking(sqt, skp, svt)
        out_b = out_b.transpose(0, 3, 1, 2).reshape(t, SB_W)

        df = df.reshape(b, s, 3, DF_HEADS, DF_DV)
        dqt = df[:, :, 0].transpose(0, 2, 3, 1)
        zq = jnp.zeros((b, DF_HEADS, DF_DQK, s), BF16)
        pad_q = jnp.zeros((b, DF_HEADS, DF_KW - 2 * DF_DQK - DF_NBIAS, s), BF16)
        bias_ones = jnp.ones((b, DF_HEADS, DF_NBIAS, s), BF16)
        qa = jnp.concatenate([dqt[:, :, :DF_DQK], zq, bias_ones, pad_q], axis=2)
        qb = jnp.concatenate([zq, dqt[:, :, DF_DQK:], bias_ones, pad_q], axis=2)
        dk = df[:, :, 1].transpose(0, 2, 1, 3)
        pad_k = jnp.zeros((b, DF_HEADS, s, DF_KW - 2 * DF_DQK - DF_NBIAS), BF16)
        kall = jnp.concatenate([dk, bias, pad_k], axis=3)
        dvt = df[:, :, 2].transpose(0, 2, 3, 1)
        pad_v = jnp.zeros((b, DF_HEADS, DF_VR - DF_DV - 1, s), BF16)
        vt = jnp.concatenate([dvt, ones_row, pad_v], axis=2)
        out_c = _diff_attention(qa, qb, kall, vt, lam_q1[l][None, :], lam_k1[l][None, :],
                                lam_q2[l][None, :], lam_k2[l][None, :], subln_g[l][:, None], lam_init)
        out_c = out_c.transpose(0, 3, 1, 2).reshape(t, DF_W)

        x1 = _outproj(x.reshape(t, d), out_a.reshape(t, W_LRU), out_b, out_c, w_out[l].astype(BF16))
        x = _ffn(x1.reshape(b, s, d), norm2_g[l][None, :], w_ff_up[l].astype(BF16), conv_ff_w[l],
                 conv_ff_b[l][None, :], w_ff_down[l].astype(BF16), final_g[None, :], l == DEPTH - 1)
    return x
```

```python
import functools

import jax
import jax.numpy as jnp
import numpy as np
from jax import lax
from jax.experimental import pallas as pl
from jax.experimental.pallas import tpu as pltpu

D_MODEL = 1024
BATCH = 4
SEQ = 8192
DEPTH = 2
CHUNK = 64
MIX = D_MODEL
W_LRU = MIX // 2
LRU_BLOCKS = 8
LRU_BW = W_LRU // LRU_BLOCKS
LRU_C = 8.0
CONV_A = 4
SB_HEADS = 4
SB_DH = MIX // 4 // SB_HEADS
SB_W = SB_HEADS * SB_DH
DF_HEADS = 4
DF_DV = MIX // 4 // DF_HEADS
DF_DQK = DF_DV // 2
DF_W = DF_HEADS * DF_DV
D_FF = ((8 * D_MODEL // 3 + 255) // 256) * 256
CONV_FF = 3
EPS = 1e-6
P_IN = 2 * W_LRU + 3 * SB_W + 3 * DF_W

SUBLANES = 8
VMEM_LIMIT = 56 * 1024 * 1024

ATT_BLK = 256
SEG = ATT_BLK // SUBLANES
NEG_BIG = -1e30
LOG2E = 1.4426950408889634
SB_EXIT = 150.0

F32 = jnp.float32
BF16 = jnp.bfloat16


def _params(sem, vmem=VMEM_LIMIT):
    return pltpu.CompilerParams(dimension_semantics=sem, vmem_limit_bytes=vmem)


def _resident(shape):
    nd = len(shape)
    return pl.BlockSpec(shape, lambda *_: (0,) * nd, pipeline_mode=pl.Buffered(1))


def _rms(x, g):
    return x * lax.rsqrt(jnp.mean(x * x, axis=-1, keepdims=True) + EPS) * g


DF_KW = 128
DF_NBIAS = 3
DF_VR = 80


def _nt_dot(w_t, h):
    return lax.dot_general(w_t, h, (((1,), (1,)), ((), ())), preferred_element_type=F32)


def _inproj_kernel(x_ref, g_ref, w_tok_ref, w_sk_ref, wt_ref, wt_sv_ref, bias_ref, ones_ref, perm_ref,
                   xy_ref, dk_ref, skp_ref, sqt_ref, dqt_ref, dvt_ref, svt_ref, *, tm):
    h = _rms(x_ref[...], g_ref[...]).astype(BF16)
    perm = perm_ref[...]
    xy_ref[...] = jnp.dot(h, w_tok_ref[:, :2 * W_LRU], preferred_element_type=F32)
    dk = jnp.dot(h, w_tok_ref[:, 2 * W_LRU:], preferred_element_type=F32)
    dk_ref[...] = (dk + bias_ref[...].astype(F32)).astype(BF16)
    sk = jnp.dot(h, w_sk_ref[...], preferred_element_type=F32).astype(BF16)
    svt = _nt_dot(wt_sv_ref[...], h).astype(BF16)
    for blk in range(tm // ATT_BLK):
        lo, hi = blk * ATT_BLK, (blk + 1) * ATT_BLK
        skp_ref[lo:hi, :] = jnp.dot(perm, sk[lo:hi, :], preferred_element_type=F32).astype(BF16)
        svt_ref[:, lo:hi] = _nt_dot(svt[:, lo:hi], perm).astype(BF16)
    sqt_ref[...] = (_nt_dot(wt_ref[:SB_W, :], h) * (SB_DH ** -0.5 * LOG2E)).astype(BF16)
    dqt_ref[...] = (_nt_dot(wt_ref[SB_W:SB_W + DF_W, :], h) * (DF_DQK ** -0.5 * LOG2E)).astype(BF16)
    dvt_ref[...] = (_nt_dot(wt_ref[SB_W + DF_W:, :], h) + ones_ref[...]).astype(BF16)


def _inproj(x, g, w, bias_tab, tm=512):
    b, s, d = x.shape
    o = 2 * W_LRU
    w_sq, w_sk, w_sv = (w[:, o + k * SB_W:o + (k + 1) * SB_W] for k in range(3))
    o += 3 * SB_W
    w_dq, w_dk, w_dv = (w[:, o + k * DF_W:o + (k + 1) * DF_W] for k in range(3))
    dk_pad = jnp.pad(w_dk.reshape(d, DF_HEADS, DF_DV), ((0, 0), (0, 0), (0, DF_KW - DF_DV)))
    dv_pad = jnp.pad(w_dv.T.reshape(DF_HEADS, DF_DV, d), ((0, 0), (0, DF_VR - DF_DV), (0, 0)))
    w_tok = jnp.concatenate([w[:, :2 * W_LRU], dk_pad.reshape(d, DF_HEADS * DF_KW)], axis=1).astype(BF16)
    wt = jnp.concatenate([w_sq.T, w_dq.T, dv_pad.reshape(DF_HEADS * DF_VR, d)], axis=0).astype(BF16)
    ones_col = np.zeros((DF_HEADS * DF_VR, 1), np.float32)
    ones_col[DF_DV::DF_VR] = 1.0
    new_row = np.arange(ATT_BLK)
    perm = np.zeros((ATT_BLK, ATT_BLK), np.float32)
    perm[new_row, (new_row % SUBLANES) * SEG + new_row // SUBLANES] = 1.0
    tok = lambda width: pl.BlockSpec((None, tm, width), lambda i, j: (i, j, 0))
    feat = lambda rows: pl.BlockSpec((None, rows, tm), lambda i, j: (i, 0, j))
    return pl.pallas_call(
        functools.partial(_inproj_kernel, tm=tm),
        grid=(b, s // tm),
        in_specs=[
            tok(d),
            _resident((1, d)),
            _resident(w_tok.shape),
            _resident((d, SB_W)),
            _resident(wt.shape),
            _resident((SB_W, d)),
            pl.BlockSpec((tm, DF_HEADS * DF_KW), lambda i, j: (j, 0)),
            _resident(ones_col.shape),
            _resident(perm.shape),
        ],
        out_specs=[tok(2 * W_LRU), tok(DF_HEADS * DF_KW), tok(SB_W),
                   feat(SB_W), feat(DF_W), feat(DF_HEADS * DF_VR), feat(SB_W)],
        out_shape=[
            jax.ShapeDtypeStruct((b, s, 2 * W_LRU), F32),
            jax.ShapeDtypeStruct((b, s, DF_HEADS * DF_KW), BF16),
            jax.ShapeDtypeStruct((b, s, SB_W), BF16),
            jax.ShapeDtypeStruct((b, SB_W, s), BF16),
            jax.ShapeDtypeStruct((b, DF_W, s), BF16),
            jax.ShapeDtypeStruct((b, DF_HEADS * DF_VR, s), BF16),
            jax.ShapeDtypeStruct((b, SB_W, s), BF16),
        ],
        compiler_params=_params(("parallel", "parallel")),
        name="inproj",
    )(x, g, w_tok, w_sk.astype(BF16), wt, w_sv.T.astype(BF16), bias_tab, jnp.asarray(ones_col),
      jnp.asarray(perm).astype(BF16))


def _softplus(x):
    return jnp.maximum(x, 0.0) + jnp.log(1.0 + jnp.exp(-jnp.abs(x)))


def _sigmoid(x):
    return 1.0 / (1.0 + jnp.exp(-x))


def _lru_kernel(xa_ref, ya_ref, cw_ref, cb_ref, wg_ref, bg_ref, lam_ref, o_ref,
                xbuf, hcar, *, ts):
    s = pl.program_id(1)

    @pl.when(s == 0)
    def _():
        xbuf[0:SUBLANES, :] = jnp.zeros((SUBLANES, W_LRU), F32)
        hcar[...] = jnp.zeros_like(hcar)

    xbuf[SUBLANES:, :] = xa_ref[...]
    xc = cb_ref[...] + cw_ref[CONV_A - 1:CONV_A, :] * xa_ref[...]
    for k in range(1, CONV_A):
        xc = xc + cw_ref[CONV_A - 1 - k:CONV_A - k, :] * xbuf[pl.ds(SUBLANES - k, ts), :]
    xbuf[0:SUBLANES, :] = xa_ref[ts - SUBLANES:, :]

    gates = jnp.dot(xc.astype(BF16), wg_ref[...], preferred_element_type=F32) + bg_ref[...]
    r = _sigmoid(gates[:, :W_LRU])
    ig = _sigmoid(gates[:, W_LRU:])
    log_a = (-LRU_C) * r * _softplus(-lam_ref[...])
    a = jnp.exp(log_a)
    u = jnp.sqrt(1.0 - jnp.exp(2.0 * log_a)) * (ig * xc)

    row = lax.broadcasted_iota(jnp.int32, (ts, W_LRU), 0)
    d = 1
    while d < ts:
        a_sh = jnp.where(row >= d, pltpu.roll(a, d, axis=0), 1.0)
        u_sh = jnp.where(row >= d, pltpu.roll(u, d, axis=0), 0.0)
        u = a * u_sh + u
        a = a * a_sh
        d *= 2
    h = a * hcar[...] + u
    hcar[...] = h[ts - 1:ts, :]
    o_ref[...] = (h * jax.nn.gelu(ya_ref[...])).astype(BF16)


def _lru(xy, cw, cb, wg_bf, bg, lam, ts=256):
    b, s, _ = xy.shape
    return pl.pallas_call(
        functools.partial(_lru_kernel, ts=ts),
        grid=(b, s // ts),
        in_specs=[
            pl.BlockSpec((None, ts, W_LRU), lambda i, j: (i, j, 0)),
            pl.BlockSpec((None, ts, W_LRU), lambda i, j: (i, j, 1)),
            _resident((CONV_A, W_LRU)),
            _resident((1, W_LRU)),
            _resident((W_LRU, 2 * W_LRU)),
            _resident((1, 2 * W_LRU)),
            _resident((1, W_LRU)),
        ],
        out_specs=pl.BlockSpec((None, ts, W_LRU), lambda i, j: (i, j, 0)),
        out_shape=jax.ShapeDtypeStruct((b, s, W_LRU), BF16),
        scratch_shapes=[
            pltpu.VMEM((SUBLANES + ts, W_LRU), F32),
            pltpu.VMEM((1, W_LRU), F32),
        ],
        compiler_params=_params(("parallel", "arbitrary")),
        name="rglru",
    )(xy, xy, cw, cb, wg_bf, bg, lam)


def _perm_local_key():
    row = lax.broadcasted_iota(jnp.int32, (ATT_BLK, ATT_BLK), 0)
    return (row & (SUBLANES - 1)) * SEG + (row >> 3)


def _softplus2(x):
    return jnp.maximum(x, 0.0) + jnp.log2(1.0 + jnp.exp2(-jnp.abs(x)))


def _sb_block(z, run, diag):
    sp = _softplus2(z)
    if diag:
        earlier = _perm_local_key() < lax.broadcasted_iota(jnp.int32, z.shape, 1)
        sp = jnp.where(earlier, sp, 0.0)
    lz = z - sp
    sp_r = [sp[r * SUBLANES:(r + 1) * SUBLANES, :] for r in range(SEG)]
    seg_tot = sp_r[0]
    for r in range(1, SEG):
        seg_tot = seg_tot + sp_r[r]
    sub = lax.broadcasted_iota(jnp.int32, seg_tot.shape, 0)
    tail = jnp.broadcast_to(run, seg_tot.shape)
    for u in range(1, SUBLANES):
        tail = tail + jnp.where(sub < u, seg_tot[u:u + 1, :], 0.0)
    ws = [None] * SEG
    for r in reversed(range(SEG)):
        ws[r] = jnp.exp2(lz[r * SUBLANES:(r + 1) * SUBLANES, :] - tail)
        tail = tail + sp_r[r]
    w = jnp.concatenate(ws, axis=0)
    if diag:
        w = jnp.where(earlier, w, 0.0)
    return w.astype(BF16), run + jnp.sum(seg_tot, axis=0, keepdims=True)


def _sb_kernel(qt_ref, k_ref, vt_ref, o_ref):
    i = pl.program_id(1)
    q_all = qt_ref[...].astype(F32)
    head_of_row = lax.broadcasted_iota(jnp.int32, q_all.shape, 0) >> 6
    qs = [jnp.where(head_of_row == h, q_all, 0.0).astype(BF16) for h in range(SB_HEADS)]

    def step(j, runs, accs, diag):
        off = pl.multiple_of(j * ATT_BLK, ATT_BLK)
        new_runs, new_accs = [], []
        kj = k_ref[pl.ds(off, ATT_BLK), :]
        zs = [jnp.dot(kj, qs[h], preferred_element_type=F32) for h in range(SB_HEADS)]
        for h in range(SB_HEADS):
            w, run = _sb_block(zs[h], runs[h], diag)
            new_runs.append(run)
            vj = vt_ref[h * SB_DH:(h + 1) * SB_DH, pl.ds(off, ATT_BLK)]
            new_accs.append(accs[h] + jnp.dot(vj, w, preferred_element_type=F32))
        return tuple(new_runs), tuple(new_accs)

    def least(runs):
        low = runs[0]
        for r in runs[1:]:
            low = jnp.minimum(low, r)
        return jnp.min(low)

    runs = tuple(jnp.zeros((1, ATT_BLK), F32) for _ in range(SB_HEADS))
    accs = tuple(jnp.zeros((SB_DH, ATT_BLK), F32) for _ in range(SB_HEADS))
    runs, accs = step(i, runs, accs, True)

    def cond(c):
        return jnp.logical_and(c[0] >= 0, c[1] <= SB_EXIT)

    def body(c):
        j, _, runs, accs = c
        runs, accs = step(j, runs, accs, False)
        return j - 1, least(runs), runs, accs

    _, _, _, accs = lax.while_loop(cond, body, (i - 1, least(runs), runs, accs))
    for h in range(SB_HEADS):
        o_ref[h * SB_DH:(h + 1) * SB_DH, :] = accs[h].astype(o_ref.dtype)


def _stick_breaking(qt, kp, vt):
    b, w, s = qt.shape
    return pl.pallas_call(
        _sb_kernel,
        grid=(b, s // ATT_BLK),
        in_specs=[
            pl.BlockSpec((None, w, ATT_BLK), lambda bi, i: (bi, 0, i)),
            pl.BlockSpec((None, s, w), lambda bi, i: (bi, 0, 0)),
            pl.BlockSpec((None, w, s), lambda bi, i: (bi, 0, 0)),
        ],
        out_specs=pl.BlockSpec((None, w, ATT_BLK), lambda bi, i: (bi, 0, i)),
        out_shape=jax.ShapeDtypeStruct((b, w, s), BF16),
        compiler_params=_params(("parallel", "arbitrary")),
        name="stickbreak",
    )(qt, kp, vt)


DF_SLOPES = tuple(2.0 ** (-8.0 * (h + 1) / DF_HEADS) for h in range(DF_HEADS))
DF_QB = 512
DF_KB = 256
DF_KPQ = DF_QB // DF_KB


def _df_kernel(q_ref, k_ref, vt_ref, lq1, lk1, lq2, lk2, gs_ref, o_ref, acc_ref, *, lam_init):
    i = pl.program_id(1)
    acc_ref[...] = jnp.zeros_like(acc_ref)
    zero = jnp.zeros((DF_DQK, DF_QB), BF16)
    rest = lax.broadcasted_iota(jnp.int32, (DF_KW - 2 * DF_DQK, DF_QB), 0)
    pick_bias = jnp.where(rest < DF_NBIAS, 1.0, 0.0).astype(BF16)
    qa, qb = [], []
    for h in range(DF_HEADS):
        q1 = q_ref[h * DF_DV:h * DF_DV + DF_DQK, :]
        q2 = q_ref[h * DF_DV + DF_DQK:(h + 1) * DF_DV, :]
        qa.append(jnp.concatenate([q1, zero, pick_bias], axis=0))
        qb.append(jnp.concatenate([zero, q2, pick_bias], axis=0))

    def update(s_, m, acc, vj):
        m_new = jnp.maximum(m, jnp.max(s_, axis=0, keepdims=True))
        p = jnp.exp2(s_ - m_new).astype(BF16)
        acc[...] = jnp.exp2(m - m_new) * acc[...] + jnp.dot(vj, p, preferred_element_type=F32)
        return m_new

    def step(j0, carry, in_tile):
        offs = [pl.multiple_of((j0 + d) * DF_KB, DF_KB) for d in range(DF_KPQ)]
        scores = []
        for off in offs:
            for h in range(DF_HEADS):
                kj = k_ref[pl.ds(off, DF_KB), h * DF_KW:(h + 1) * DF_KW]
                scores.append((jnp.dot(kj, qa[h], preferred_element_type=F32),
                               jnp.dot(kj, qb[h], preferred_element_type=F32)))
        carry = list(carry)
        for d, off in enumerate(offs):
            if in_tile:
                kpos = lax.broadcasted_iota(jnp.int32, (DF_KB, DF_QB), 0) + d * DF_KB
                qpos = lax.broadcasted_iota(jnp.int32, (DF_KB, DF_QB), 1)
                ahead = jnp.maximum(kpos - qpos, 0).astype(F32)
                allowed = (kpos >> 6) <= (qpos >> 6)
            for h in range(DF_HEADS):
                m1, m2 = carry[h]
                vj = vt_ref[h * DF_VR:(h + 1) * DF_VR, pl.ds(off, DF_KB)]
                s1, s2 = scores[d * DF_HEADS + h]
                if in_tile:
                    fix = (2.0 * DF_SLOPES[h] * LOG2E) * ahead
                    s1 = jnp.where(allowed, s1 - fix, NEG_BIG)
                    s2 = jnp.where(allowed, s2 - fix, NEG_BIG)
                carry[h] = (update(s1, m1, acc_ref.at[h, 0], vj), update(s2, m2, acc_ref.at[h, 1], vj))
        return tuple(carry)

    m0 = jnp.full((1, DF_QB), NEG_BIG, F32)
    carry = tuple((m0, m0) for _ in range(DF_HEADS))
    carry = step(i * DF_KPQ, carry, True)
    carry = lax.fori_loop(0, i, lambda t, c: step(t * DF_KPQ, c, False), carry)

    lam = (jnp.exp(jnp.sum(lq1[...] * lk1[...], axis=-1, keepdims=True))
           - jnp.exp(jnp.sum(lq2[...] * lk2[...], axis=-1, keepdims=True)) + lam_init)
    for h in range(DF_HEADS):
        a1 = acc_ref[h, 0]
        a2 = acc_ref[h, 1]
        o = a1[:DF_DV, :] / a1[DF_DV:DF_DV + 1, :] - lam * (a2[:DF_DV, :] / a2[DF_DV:DF_DV + 1, :])
        o = o * lax.rsqrt(jnp.mean(o * o, axis=0, keepdims=True) + EPS) * gs_ref[...]
        o_ref[h * DF_DV:(h + 1) * DF_DV, :] = (o * (1.0 - lam_init)).astype(o_ref.dtype)


def _diff_attention(qt, kall, vt, lq1, lk1, lq2, lk2, gs, lam_init):
    b, w, s = qt.shape
    vec = _resident((1, DF_DQK))
    return pl.pallas_call(
        functools.partial(_df_kernel, lam_init=lam_init),
        grid=(b, s // DF_QB),
        in_specs=[
            pl.BlockSpec((None, w, DF_QB), lambda bi, i: (bi, 0, i)),
            pl.BlockSpec((None, s, DF_HEADS * DF_KW), lambda bi, i: (bi, 0, 0)),
            pl.BlockSpec((None, DF_HEADS * DF_VR, s), lambda bi, i: (bi, 0, 0)),
            vec, vec, vec, vec,
            _resident((DF_DV, 1)),
        ],
        out_specs=pl.BlockSpec((None, w, DF_QB), lambda bi, i: (bi, 0, i)),
        out_shape=jax.ShapeDtypeStruct((b, w, s), BF16),
        scratch_shapes=[pltpu.VMEM((DF_HEADS, 2, DF_VR, DF_QB), F32)],
        compiler_params=_params(("parallel", "arbitrary")),
        name="diffattn",
    )(qt, kall, vt, lq1, lk1, lq2, lk2, gs)


def _tn_dot(a_t, w):
    return lax.dot_general(a_t, w, (((0,), (0,)), ((), ())), preferred_element_type=F32)


def _outproj_kernel(x_ref, a_ref, bt_ref, ct_ref, w_ref, o_ref):
    acc = jnp.dot(a_ref[...], w_ref[:W_LRU, :], preferred_element_type=F32)
    acc = acc + _tn_dot(bt_ref[...], w_ref[W_LRU:W_LRU + SB_W, :])
    acc = acc + _tn_dot(ct_ref[...], w_ref[W_LRU + SB_W:, :])
    o_ref[...] = x_ref[...] + acc


def _outproj(x, oa, obt, oct, w_bf, tm=512):
    b, s, d = x.shape
    tok = lambda width: pl.BlockSpec((None, tm, width), lambda i, j: (i, j, 0))
    feat = lambda rows: pl.BlockSpec((None, rows, tm), lambda i, j: (i, 0, j))
    return pl.pallas_call(
        _outproj_kernel,
        grid=(b, s // tm),
        in_specs=[tok(d), tok(W_LRU), feat(SB_W), feat(DF_W), _resident((MIX, d))],
        out_specs=tok(d),
        out_shape=jax.ShapeDtypeStruct((b, s, d), F32),
        compiler_params=_params(("parallel", "parallel")),
        name="outproj",
    )(x, oa, obt, oct, w_bf)


FF_CHUNK = 256
FF_NCHUNK = D_FF // FF_CHUNK


def _ffn_kernel(x_ref, g_ref, wu_ref, cw_ref, cb_ref, wd_ref, fg_ref, o_ref,
                ubuf, tails, acc_ref, *, tm, final_norm):
    s = pl.program_id(1)

    @pl.when(s == 0)
    def _():
        tails[...] = jnp.zeros_like(tails)

    x = x_ref[...]
    h = _rms(x, g_ref[...]).astype(BF16)

    def up(c):
        return tuple(jnp.dot(h, wu_ref[:, pl.ds(col, FF_CHUNK)], preferred_element_type=F32)
                     for col in (c * FF_CHUNK, D_FF + c * FF_CHUNK))

    def conv(c, col, u):
        ubuf[0:SUBLANES, :] = tails[c]
        ubuf[SUBLANES:, :] = u
        tails[c] = u[tm - SUBLANES:, :]
        w = cw_ref[:, pl.ds(col, FF_CHUNK)]
        y = cb_ref[:, pl.ds(col, FF_CHUNK)] + w[CONV_FF - 1:CONV_FF, :] * u
        for k in range(1, CONV_FF):
            y = y + w[CONV_FF - 1 - k:CONV_FF - k, :] * ubuf[pl.ds(SUBLANES - k, tm), :]
        return y

    nxt = up(0)
    for c in range(FF_NCHUNK):
        ug, uv = nxt
        if c + 1 < FF_NCHUNK:
            nxt = up(c + 1)
        gate = conv(2 * c, c * FF_CHUNK, ug)
        val = conv(2 * c + 1, D_FF + c * FF_CHUNK, uv)
        act = (gate * _sigmoid(gate) * val).astype(BF16)
        part = jnp.dot(act, wd_ref[c * FF_CHUNK:(c + 1) * FF_CHUNK, :], preferred_element_type=F32)
        if c == 0:
            acc_ref[...] = part
        else:
            acc_ref[...] += part
    y = x + acc_ref[...]
    if final_norm:
        y = _rms(y, fg_ref[...])
    o_ref[...] = y


def _ffn(x3, g, wu_bf, cw, cb, wd_bf, fg, final_norm, tm=512):
    b, s, _ = x3.shape
    return pl.pallas_call(
        functools.partial(_ffn_kernel, tm=tm, final_norm=final_norm),
        grid=(b, s // tm),
        in_specs=[
            pl.BlockSpec((None, tm, D_MODEL), lambda i, j: (i, j, 0)),
            _resident((1, D_MODEL)),
            _resident((D_MODEL, 2 * D_FF)),
            _resident((CONV_FF, 2 * D_FF)),
            _resident((1, 2 * D_FF)),
            _resident((D_FF, D_MODEL)),
            _resident((1, D_MODEL)),
        ],
        out_specs=pl.BlockSpec((None, tm, D_MODEL), lambda i, j: (i, j, 0)),
        out_shape=jax.ShapeDtypeStruct((b, s, D_MODEL), F32),
        scratch_shapes=[
            pltpu.VMEM((SUBLANES + tm, FF_CHUNK), F32),
            pltpu.VMEM((2 * FF_NCHUNK, SUBLANES, FF_CHUNK), F32),
            pltpu.VMEM((tm, D_MODEL), F32),
        ],
        compiler_params=_params(("parallel", "arbitrary")),
        name="convffn",
    )(x3, g, wu_bf, cw, cb, wd_bf, fg)


def _alibi_bias_table(s):
    resid = (np.asarray(DF_SLOPES, np.float32)[:, None] * np.float32(LOG2E)
             * np.arange(s, dtype=np.float32)[None, :]).astype(np.float32)
    terms = []
    for _ in range(DF_NBIAS):
        top = (resid.view(np.uint32) & np.uint32(0xFFFF0000)).view(np.float32)
        terms.append(top)
        resid = resid - top
    assert not resid.any()
    table = np.zeros((s, DF_HEADS, DF_KW), np.float32)
    table[:, :, 2 * DF_DQK:2 * DF_DQK + DF_NBIAS] = np.stack(terms, axis=-1).transpose(1, 0, 2)
    return jnp.asarray(table.reshape(s, DF_HEADS * DF_KW)).astype(BF16)


def _block_diag(w):
    n, bw, _ = w.shape
    eye = jnp.eye(n, dtype=w.dtype)
    return jnp.einsum("ncd,nm->ncmd", w, eye).reshape(n * bw, n * bw)


def kernel(x, norm1_g, w_in, conv_a_w, conv_a_b, w_rgate, b_rgate, w_igate, b_igate, lru_lambda, lam_q1, lam_k1, lam_q2, lam_k2, subln_g, w_out, norm2_g, w_ff_up, conv_ff_w, conv_ff_b, w_ff_down, final_g):
    s = x.shape[1]
    bias_tab = _alibi_bias_table(s)
    for l in range(DEPTH):
        lam_init = 0.8 - 0.6 * float(np.exp(-0.3 * l))
        xy, dk, skp, sqt, dqt, dvt, svt = _inproj(x, norm1_g[l][None, :], w_in[l], bias_tab)

        wg = jnp.concatenate([_block_diag(w_rgate[l]), _block_diag(w_igate[l])], axis=1).astype(BF16)
        bg = jnp.concatenate([b_rgate[l], b_igate[l]])[None, :]
        out_a = _lru(xy, conv_a_w[l], conv_a_b[l][None, :], wg, bg, lru_lambda[l][None, :])
        out_bt = _stick_breaking(sqt, skp, svt)
        out_ct = _diff_attention(dqt, dk, dvt, lam_q1[l][None, :], lam_k1[l][None, :],
                                 lam_q2[l][None, :], lam_k2[l][None, :], subln_g[l][:, None], lam_init)

        x1 = _outproj(x, out_a, out_bt, out_ct, w_out[l].astype(BF16))
        x = _ffn(x1, norm2_g[l][None, :], w_ff_up[l].astype(BF16), conv_ff_w[l],
                 conv_ff_b[l][None, :], w_ff_down[l].astype(BF16), final_g[None, :], l == DEPTH - 1)
    return x
```

```python
import functools

import jax
import jax.numpy as jnp
import numpy as np
from jax import lax
from jax.experimental import pallas as pl
from jax.experimental.pallas import tpu as pltpu

D_MODEL = 1024
BATCH = 4
SEQ = 8192
DEPTH = 2
CHUNK = 64
MIX = D_MODEL
W_LRU = MIX // 2
LRU_BLOCKS = 8
LRU_BW = W_LRU // LRU_BLOCKS
LRU_C = 8.0
CONV_A = 4
SB_HEADS = 4
SB_DH = MIX // 4 // SB_HEADS
SB_W = SB_HEADS * SB_DH
DF_HEADS = 4
DF_DV = MIX // 4 // DF_HEADS
DF_DQK = DF_DV // 2
DF_W = DF_HEADS * DF_DV
D_FF = ((8 * D_MODEL // 3 + 255) // 256) * 256
CONV_FF = 3
EPS = 1e-6
P_IN = 2 * W_LRU + 3 * SB_W + 3 * DF_W

SUBLANES = 8
VMEM_LIMIT = 56 * 1024 * 1024

ATT_BLK = 256
SEG = ATT_BLK // SUBLANES
NEG_BIG = -1e30
LOG2E = 1.4426950408889634

F32 = jnp.float32
BF16 = jnp.bfloat16


def _params(sem, vmem=VMEM_LIMIT):
    return pltpu.CompilerParams(dimension_semantics=sem, vmem_limit_bytes=vmem)


def _resident(shape):
    nd = len(shape)
    return pl.BlockSpec(shape, lambda *_: (0,) * nd, pipeline_mode=pl.Buffered(1))


def _rms(x, g):
    return x * lax.rsqrt(jnp.mean(x * x, axis=-1, keepdims=True) + EPS) * g


DF_KW = 128
DF_NBIAS = 3
DF_VR = 80


def _nt_dot(w_t, h):
    return lax.dot_general(w_t, h, (((1,), (1,)), ((), ())), preferred_element_type=F32)


def _inproj_kernel(x_ref, g_ref, w_tok_ref, w_sk_ref, wt_ref, wt_sv_ref, bias_ref, ones_ref, perm_ref,
                   sel_ref, xy_ref, dk_ref, skp_ref, sqt_ref, dqt_ref, dvt_ref, svt_ref, kn_ref, *, tm):
    h = _rms(x_ref[...], g_ref[...]).astype(BF16)
    perm = perm_ref[...]
    xy_ref[...] = jnp.dot(h, w_tok_ref[:, :2 * W_LRU], preferred_element_type=F32)
    dk = jnp.dot(h, w_tok_ref[:, 2 * W_LRU:], preferred_element_type=F32).astype(BF16)
    dk_ref[...] = dk + bias_ref[...]
    dkf = dk.astype(F32)
    norms = jnp.dot((dkf * dkf).astype(BF16), sel_ref[...], preferred_element_type=F32)
    kn_ref[...] = jnp.broadcast_to(jnp.max(norms, axis=0, keepdims=True), kn_ref.shape)
    sk = jnp.dot(h, w_sk_ref[...], preferred_element_type=F32).astype(BF16)
    svt = _nt_dot(wt_sv_ref[...], h).astype(BF16)
    for blk in range(tm // ATT_BLK):
        lo, hi = blk * ATT_BLK, (blk + 1) * ATT_BLK
        skp_ref[lo:hi, :] = jnp.dot(perm, sk[lo:hi, :], preferred_element_type=F32).astype(BF16)
        svt_ref[:, lo:hi] = _nt_dot(svt[:, lo:hi], perm).astype(BF16)
    sqt_ref[...] = (_nt_dot(wt_ref[:SB_W, :], h) * (-(SB_DH ** -0.5) * LOG2E)).astype(BF16)
    dqt_ref[...] = (_nt_dot(wt_ref[SB_W:SB_W + DF_W, :], h) * (DF_DQK ** -0.5 * LOG2E)).astype(BF16)
    dvt_ref[...] = (_nt_dot(wt_ref[SB_W + DF_W:, :], h) + ones_ref[...]).astype(BF16)


def _inproj(x, g, w, bias_tab, tm=512):
    b, s, d = x.shape
    o = 2 * W_LRU
    w_sq, w_sk, w_sv = (w[:, o + k * SB_W:o + (k + 1) * SB_W] for k in range(3))
    o += 3 * SB_W
    w_dq, w_dk, w_dv = (w[:, o + k * DF_W:o + (k + 1) * DF_W] for k in range(3))
    dk_pad = jnp.pad(w_dk.reshape(d, DF_HEADS, DF_DV), ((0, 0), (0, 0), (0, DF_KW - DF_DV)))
    dv_pad = jnp.pad(w_dv.T.reshape(DF_HEADS, DF_DV, d), ((0, 0), (0, DF_VR - DF_DV), (0, 0)))
    w_tok = jnp.concatenate([w[:, :2 * W_LRU], dk_pad.reshape(d, DF_HEADS * DF_KW)], axis=1).astype(BF16)
    wt = jnp.concatenate([w_sq.T, w_dq.T, dv_pad.reshape(DF_HEADS * DF_VR, d)], axis=0).astype(BF16)
    ones_col = np.zeros((DF_HEADS * DF_VR, 1), np.float32)
    ones_col[DF_DV::DF_VR] = 1.0
    new_row = np.arange(ATT_BLK)
    perm = np.zeros((ATT_BLK, ATT_BLK), np.float32)
    perm[new_row, (new_row % SUBLANES) * SEG + new_row // SUBLANES] = 1.0
    sel = np.zeros((DF_HEADS, DF_KW, DF_KW), np.float32)
    for h in range(DF_HEADS):
        for mp in range(2):
            sel[h, mp * DF_DQK:(mp + 1) * DF_DQK, 2 * h + mp] = 1.0
    sel = sel.reshape(DF_HEADS * DF_KW, DF_KW)
    assert tm == DF_QB
    tok = lambda width: pl.BlockSpec((None, tm, width), lambda i, j: (i, j, 0))
    feat = lambda rows: pl.BlockSpec((None, rows, tm), lambda i, j: (i, 0, j))
    return pl.pallas_call(
        functools.partial(_inproj_kernel, tm=tm),
        grid=(b, s // tm),
        in_specs=[
            tok(d),
            _resident((1, d)),
            _resident(w_tok.shape),
            _resident((d, SB_W)),
            _resident(wt.shape),
            _resident((SB_W, d)),
            pl.BlockSpec((tm, DF_HEADS * DF_KW), lambda i, j: (j, 0)),
            _resident(ones_col.shape),
            _resident(perm.shape),
            _resident(sel.shape),
        ],
        out_specs=[tok(2 * W_LRU), tok(DF_HEADS * DF_KW), tok(SB_W),
                   feat(SB_W), feat(DF_W), feat(DF_HEADS * DF_VR), feat(SB_W),
                   pl.BlockSpec((None, None, SUBLANES, DF_KW), lambda i, j: (i, j, 0, 0))],
        out_shape=[
            jax.ShapeDtypeStruct((b, s, 2 * W_LRU), F32),
            jax.ShapeDtypeStruct((b, s, DF_HEADS * DF_KW), BF16),
            jax.ShapeDtypeStruct((b, s, SB_W), BF16),
            jax.ShapeDtypeStruct((b, SB_W, s), BF16),
            jax.ShapeDtypeStruct((b, DF_W, s), BF16),
            jax.ShapeDtypeStruct((b, DF_HEADS * DF_VR, s), BF16),
            jax.ShapeDtypeStruct((b, SB_W, s), BF16),
            jax.ShapeDtypeStruct((b, s // tm, SUBLANES, DF_KW), F32),
        ],
        compiler_params=_params(("parallel", "parallel")),
        name="inproj",
    )(x, g, w_tok, w_sk.astype(BF16), wt, w_sv.T.astype(BF16), bias_tab, jnp.asarray(ones_col),
      jnp.asarray(perm).astype(BF16), jnp.asarray(sel).astype(BF16))


def _softplus(x):
    return jnp.maximum(x, 0.0) + jnp.log(1.0 + jnp.exp(-jnp.abs(x)))


def _sigmoid(x):
    return 1.0 / (1.0 + jnp.exp(-x))


def _lru_kernel(xa_ref, ya_ref, cw_ref, cb_ref, wg_ref, bg_ref, lam_ref, o_ref,
                xbuf, hcar, *, ts):
    s = pl.program_id(1)

    @pl.when(s == 0)
    def _():
        xbuf[0:SUBLANES, :] = jnp.zeros((SUBLANES, W_LRU), F32)
        hcar[...] = jnp.zeros_like(hcar)

    xbuf[SUBLANES:, :] = xa_ref[...]
    xc = cb_ref[...] + cw_ref[CONV_A - 1:CONV_A, :] * xa_ref[...]
    for k in range(1, CONV_A):
        xc = xc + cw_ref[CONV_A - 1 - k:CONV_A - k, :] * xbuf[pl.ds(SUBLANES - k, ts), :]
    xbuf[0:SUBLANES, :] = xa_ref[ts - SUBLANES:, :]

    gates = jnp.dot(xc.astype(BF16), wg_ref[...], preferred_element_type=F32) + bg_ref[...]
    r = _sigmoid(gates[:, :W_LRU])
    ig = _sigmoid(gates[:, W_LRU:])
    log_a = (-LRU_C) * r * _softplus(-lam_ref[...])
    a = jnp.exp(log_a)
    u = jnp.sqrt(1.0 - a * a) * (ig * xc)

    groups = ts // SUBLANES
    a = a.reshape(groups, SUBLANES, W_LRU)
    u = u.reshape(groups, SUBLANES, W_LRU)
    sub = lax.broadcasted_iota(jnp.int32, a.shape, 1)
    d = 1
    while d < SUBLANES:
        a_sh = jnp.where(sub >= d, pltpu.roll(a, d, axis=1), 1.0)
        u_sh = jnp.where(sub >= d, pltpu.roll(u, d, axis=1), 0.0)
        u = a * u_sh + u
        a = a * a_sh
        d *= 2
    carry = hcar[...]
    hs = []
    for grp in range(groups):
        hs.append(a[grp] * carry + u[grp])
        carry = hs[-1][SUBLANES - 1:SUBLANES, :]
    hcar[...] = carry
    o_ref[...] = (jnp.concatenate(hs, axis=0) * jax.nn.gelu(ya_ref[...])).astype(BF16)


def _lru(xy, cw, cb, wg_bf, bg, lam, ts=256):
    b, s, _ = xy.shape
    return pl.pallas_call(
        functools.partial(_lru_kernel, ts=ts),
        grid=(b, s // ts),
        in_specs=[
            pl.BlockSpec((None, ts, W_LRU), lambda i, j: (i, j, 0)),
            pl.BlockSpec((None, ts, W_LRU), lambda i, j: (i, j, 1)),
            _resident((CONV_A, W_LRU)),
            _resident((1, W_LRU)),
            _resident((W_LRU, 2 * W_LRU)),
            _resident((1, 2 * W_LRU)),
            _resident((1, W_LRU)),
        ],
        out_specs=pl.BlockSpec((None, ts, W_LRU), lambda i, j: (i, j, 0)),
        out_shape=jax.ShapeDtypeStruct((b, s, W_LRU), BF16),
        scratch_shapes=[
            pltpu.VMEM((SUBLANES + ts, W_LRU), F32),
            pltpu.VMEM((1, W_LRU), F32),
        ],
        compiler_params=_params(("parallel", "arbitrary")),
        name="rglru",
    )(xy, xy, cw, cb, wg_bf, bg, lam)


def _perm_local_key():
    row = lax.broadcasted_iota(jnp.int32, (ATT_BLK, ATT_BLK), 0)
    return (row & (SUBLANES - 1)) * SEG + (row >> 3)


def _sb_block(zneg, run, diag):
    beta = 1.0 / (1.0 + jnp.exp2(zneg))
    om = 1.0 - beta
    if diag:
        earlier = _perm_local_key() < lax.broadcasted_iota(jnp.int32, zneg.shape, 1)
        om = jnp.where(earlier, om, 1.0)
    om_r = [om[r * SUBLANES:(r + 1) * SUBLANES, :] for r in range(SEG)]
    seg_tot = om_r[0]
    for r in range(1, SEG):
        seg_tot = seg_tot * om_r[r]
    sub = lax.broadcasted_iota(jnp.int32, seg_tot.shape, 0)
    tail = jnp.broadcast_to(run, seg_tot.shape)
    for u in range(1, SUBLANES):
        tail = tail * jnp.where(sub < u, seg_tot[u:u + 1, :], 1.0)
    ws = [None] * SEG
    for r in reversed(range(SEG)):
        ws[r] = beta[r * SUBLANES:(r + 1) * SUBLANES, :] * tail
        tail = tail * om_r[r]
    w = jnp.concatenate(ws, axis=0)
    if diag:
        w = jnp.where(earlier, w, 0.0)
    return w.astype(BF16), tail[0:1, :]


def _sb_kernel(qt_ref, k_ref, vt_ref, o_ref):
    i = pl.program_id(1)
    q_all = qt_ref[...].astype(F32)
    head_of_row = lax.broadcasted_iota(jnp.int32, q_all.shape, 0) >> 6
    qs = [jnp.where(head_of_row == h, q_all, 0.0).astype(BF16) for h in range(SB_HEADS)]

    def step(j, runs, accs, diag):
        off = pl.multiple_of(j * ATT_BLK, ATT_BLK)
        new_runs, new_accs = [], []
        kj = k_ref[pl.ds(off, ATT_BLK), :]
        zs = [jnp.dot(kj, qs[h], preferred_element_type=F32) for h in range(SB_HEADS)]
        for h in range(SB_HEADS):
            w, run = _sb_block(zs[h], runs[h], diag)
            new_runs.append(run)
            vj = vt_ref[h * SB_DH:(h + 1) * SB_DH, pl.ds(off, ATT_BLK)]
            new_accs.append(accs[h] + jnp.dot(vj, w, preferred_element_type=F32))
        return tuple(new_runs), tuple(new_accs)

    def largest(runs):
        top = runs[0]
        for r in runs[1:]:
            top = jnp.maximum(top, r)
        return jnp.max(top)

    runs = tuple(jnp.ones((1, ATT_BLK), F32) for _ in range(SB_HEADS))
    accs = tuple(jnp.zeros((SB_DH, ATT_BLK), F32) for _ in range(SB_HEADS))
    runs, accs = step(i, runs, accs, True)

    def cond(c):
        return jnp.logical_and(c[0] >= 0, c[1] > 0.0)

    def body(c):
        j, _, runs, accs = c
        runs, accs = step(j, runs, accs, False)
        return j - 1, largest(runs), runs, accs

    _, _, _, accs = lax.while_loop(cond, body, (i - 1, largest(runs), runs, accs))
    for h in range(SB_HEADS):
        o_ref[h * SB_DH:(h + 1) * SB_DH, :] = accs[h].astype(o_ref.dtype)


def _stick_breaking(qt, kp, vt):
    b, w, s = qt.shape
    return pl.pallas_call(
        _sb_kernel,
        grid=(b, s // ATT_BLK),
        in_specs=[
            pl.BlockSpec((None, w, ATT_BLK), lambda bi, i: (bi, 0, i)),
            pl.BlockSpec((None, s, w), lambda bi, i: (bi, 0, 0)),
            pl.BlockSpec((None, w, s), lambda bi, i: (bi, 0, 0)),
        ],
        out_specs=pl.BlockSpec((None, w, ATT_BLK), lambda bi, i: (bi, 0, i)),
        out_shape=jax.ShapeDtypeStruct((b, w, s), BF16),
        compiler_params=_params(("parallel", "arbitrary")),
        name="stickbreak",
    )(qt, kp, vt)


DF_SLOPES = tuple(2.0 ** (-8.0 * (h + 1) / DF_HEADS) for h in range(DF_HEADS))
DF_QB = 512
DF_KB = 256
DF_KPQ = DF_QB // DF_KB
DF_SKIP = 150.0
DF_GROUPS = ((0, 1), (2, 3))


def _df_kernel(q_ref, k_ref, vt_ref, kn_ref, lq1, lk1, lq2, lk2, gs_ref, o_ref, acc_ref, kp_ref,
               *, lam_init):
    i = pl.program_id(1)
    acc_ref[...] = jnp.zeros_like(acc_ref)
    zero = jnp.zeros((DF_DQK, DF_QB), BF16)
    rest = lax.broadcasted_iota(jnp.int32, (DF_KW - 2 * DF_DQK, DF_QB), 0)
    pick_bias = jnp.where(rest < DF_NBIAS, 1.0, 0.0).astype(BF16)
    qa, qb = [], []
    for h in range(DF_HEADS):
        q1 = q_ref[h * DF_DV:h * DF_DV + DF_DQK, :]
        q2 = q_ref[h * DF_DV + DF_DQK:(h + 1) * DF_DV, :]
        qa.append(jnp.concatenate([q1, zero, pick_bias], axis=0))
        qb.append(jnp.concatenate([zero, q2, pick_bias], axis=0))

    def update(s_, m, acc, vj):
        m_new = jnp.maximum(m, jnp.max(s_, axis=0, keepdims=True))
        p = jnp.exp2(s_ - m_new).astype(BF16)
        acc[...] = jnp.exp2(m - m_new) * acc[...] + jnp.dot(vj, p, preferred_element_type=F32)
        return m_new

    def step(j0, ms, heads, in_tile):
        offs = [pl.multiple_of((j0 + d) * DF_KB, DF_KB) for d in range(DF_KPQ)]
        scores = []
        for off in offs:
            for h in heads:
                kj = k_ref[pl.ds(off, DF_KB), h * DF_KW:(h + 1) * DF_KW]
                scores.append((jnp.dot(kj, qa[h], preferred_element_type=F32),
                               jnp.dot(kj, qb[h], preferred_element_type=F32)))
        ms = list(ms)
        for d, off in enumerate(offs):
            if in_tile:
                kpos = lax.broadcasted_iota(jnp.int32, (DF_KB, DF_QB), 0) + d * DF_KB
                qpos = lax.broadcasted_iota(jnp.int32, (DF_KB, DF_QB), 1)
                ahead = jnp.maximum(kpos - qpos, 0).astype(F32)
                allowed = (kpos >> 6) <= (qpos >> 6)
            for n, h in enumerate(heads):
                m1, m2 = ms[n]
                vj = vt_ref[h * DF_VR:(h + 1) * DF_VR, pl.ds(off, DF_KB)]
                s1, s2 = scores[d * len(heads) + n]
                if in_tile:
                    fix = (2.0 * DF_SLOPES[h] * LOG2E) * ahead
                    s1 = jnp.where(allowed, s1 - fix, NEG_BIG)
                    s2 = jnp.where(allowed, s2 - fix, NEG_BIG)
                ms[n] = (update(s1, m1, acc_ref.at[h, 0], vj), update(s2, m2, acc_ref.at[h, 1], vj))
        return tuple(ms)

    lane = lax.broadcasted_iota(jnp.int32, (1, DF_KW), 1)

    def per_map(values):
        out = jnp.zeros((1, DF_KW), F32)
        for h, pair in values.items():
            for mp, v in enumerate(pair):
                out = jnp.where(lane == 2 * h + mp, v, out)
        return out

    def sq_norm(q):
        sq = q.astype(F32)
        return jnp.max(jnp.sum(sq * sq, axis=0, keepdims=True), axis=1, keepdims=True)

    qn2 = per_map({h: (sq_norm(qa[h][:DF_DQK, :]), sq_norm(qb[h][DF_DQK:2 * DF_DQK, :]))
                   for h in range(DF_HEADS)})
    slope2 = per_map({h: (jnp.full((1, 1), DF_SLOPES[h] * LOG2E, F32),) * 2 for h in range(DF_HEADS)})
    top = kn_ref[0:1, :]
    kp_rows = [top]
    for r in range(1, kn_ref.shape[0]):
        top = jnp.maximum(top, kn_ref[r:r + 1, :])
        kp_rows.append(top)
    kp_ref[...] = jnp.concatenate(kp_rows, axis=0)

    def needed(jt, ms, heads):
        jt = jnp.maximum(jt, 0)
        low = per_map({h: tuple(jnp.min(m, axis=1, keepdims=True) for m in ms[n])
                       for n, h in enumerate(heads)})
        last_key = (jnp.full((1, DF_KW), jt, jnp.int32) * DF_QB + (DF_QB - 1)).astype(F32)
        room = low - (DF_SKIP + 1.0) - slope2 * last_key
        reach = jnp.logical_or(room <= 0.0, qn2 * kp_ref[pl.ds(jt, 1), :] * 1.02 >= room * room)
        mine = jnp.logical_and(lane >= 2 * heads[0], lane < 2 * heads[-1] + 2)
        return jnp.max(jnp.where(jnp.logical_and(mine, reach), 1.0, 0.0))

    m0 = jnp.full((1, DF_QB), NEG_BIG, F32)
    ms_all = step(i * DF_KPQ, tuple((m0, m0) for _ in range(DF_HEADS)), tuple(range(DF_HEADS)), True)
    for heads in DF_GROUPS:
        def cond(c):
            return jnp.logical_and(c[0] >= 0, c[1] > 0.5)

        def body(c, heads=heads):
            jt, _, ms = c
            ms = step(jt * DF_KPQ, ms, heads, False)
            return jt - 1, needed(jt - 1, ms, heads), ms

        ms = tuple(ms_all[h] for h in heads)
        lax.while_loop(cond, body, (i - 1, needed(i - 1, ms, heads), ms))

    lam = (jnp.exp(jnp.sum(lq1[...] * lk1[...], axis=-1, keepdims=True))
           - jnp.exp(jnp.sum(lq2[...] * lk2[...], axis=-1, keepdims=True)) + lam_init)
    for h in range(DF_HEADS):
        a1 = acc_ref[h, 0]
        a2 = acc_ref[h, 1]
        o = a1[:DF_DV, :] / a1[DF_DV:DF_DV + 1, :] - lam * (a2[:DF_DV, :] / a2[DF_DV:DF_DV + 1, :])
        o = o * lax.rsqrt(jnp.mean(o * o, axis=0, keepdims=True) + EPS) * gs_ref[...]
        o_ref[h * DF_DV:(h + 1) * DF_DV, :] = (o * (1.0 - lam_init)).astype(o_ref.dtype)


def _diff_attention(qt, kall, vt, kn, lq1, lk1, lq2, lk2, gs, lam_init):
    b, w, s = qt.shape
    vec = _resident((1, DF_DQK))
    return pl.pallas_call(
        functools.partial(_df_kernel, lam_init=lam_init),
        grid=(b, s // DF_QB),
        in_specs=[
            pl.BlockSpec((None, w, DF_QB), lambda bi, i: (bi, 0, i)),
            pl.BlockSpec((None, s, DF_HEADS * DF_KW), lambda bi, i: (bi, 0, 0)),
            pl.BlockSpec((None, DF_HEADS * DF_VR, s), lambda bi, i: (bi, 0, 0)),
            pl.BlockSpec((None, s // DF_QB, DF_KW), lambda bi, i: (bi, 0, 0)),
            vec, vec, vec, vec,
            _resident((DF_DV, 1)),
        ],
        out_specs=pl.BlockSpec((None, w, DF_QB), lambda bi, i: (bi, 0, i)),
        out_shape=jax.ShapeDtypeStruct((b, w, s), BF16),
        scratch_shapes=[pltpu.VMEM((DF_HEADS, 2, DF_VR, DF_QB), F32),
                        pltpu.VMEM((s // DF_QB, DF_KW), F32)],
        compiler_params=_params(("parallel", "arbitrary")),
        name="diffattn",
    )(qt, kall, vt, kn, lq1, lk1, lq2, lk2, gs)


FF_CHUNK = 256
FF_NCHUNK = D_FF // FF_CHUNK


def _tn_dot(a_t, w):
    return lax.dot_general(a_t, w, (((0,), (0,)), ((), ())), preferred_element_type=F32)


def _ffn_kernel(x_ref, a_ref, bt_ref, ct_ref, wo_ref, g_ref, wu_ref, cw_ref, cb_ref, wd_ref, fg_ref,
                o_ref, ubuf, tails, acc_ref, *, tm, final_norm):
    s = pl.program_id(1)

    @pl.when(s == 0)
    def _():
        tails[...] = jnp.zeros_like(tails)

    mix = jnp.dot(a_ref[...], wo_ref[:W_LRU, :], preferred_element_type=F32)
    mix = mix + _tn_dot(bt_ref[...], wo_ref[W_LRU:W_LRU + SB_W, :])
    mix = mix + _tn_dot(ct_ref[...], wo_ref[W_LRU + SB_W:, :])
    x = x_ref[...] + mix
    h = _rms(x, g_ref[...]).astype(BF16)

    def up(c):
        return tuple(jnp.dot(h, wu_ref[:, pl.ds(col, FF_CHUNK)], preferred_element_type=F32)
                     for col in (c * FF_CHUNK, D_FF + c * FF_CHUNK))

    def conv(c, col, u):
        ubuf[0:SUBLANES, :] = tails[c]
        ubuf[SUBLANES:, :] = u
        tails[c] = u[tm - SUBLANES:, :]
        w = cw_ref[:, pl.ds(col, FF_CHUNK)]
        y = cb_ref[:, pl.ds(col, FF_CHUNK)] + w[CONV_FF - 1:CONV_FF, :] * u
        for k in range(1, CONV_FF):
            y = y + w[CONV_FF - 1 - k:CONV_FF - k, :] * ubuf[pl.ds(SUBLANES - k, tm), :]
        return y

    nxt = up(0)
    for c in range(FF_NCHUNK):
        ug, uv = nxt
        if c + 1 < FF_NCHUNK:
            nxt = up(c + 1)
        gate = conv(2 * c, c * FF_CHUNK, ug)
        val = conv(2 * c + 1, D_FF + c * FF_CHUNK, uv)
        act = (gate * _sigmoid(gate) * val).astype(BF16)
        part = jnp.dot(act, wd_ref[c * FF_CHUNK:(c + 1) * FF_CHUNK, :], preferred_element_type=F32)
        if c == 0:
            acc_ref[...] = part
        else:
            acc_ref[...] += part
    y = x + acc_ref[...]
    if final_norm:
        y = _rms(y, fg_ref[...])
    o_ref[...] = y


def _ffn(x3, oa, obt, oct, wo_bf, g, wu_bf, cw, cb, wd_bf, fg, final_norm, tm=512):
    b, s, _ = x3.shape
    tok = lambda width: pl.BlockSpec((None, tm, width), lambda i, j: (i, j, 0))
    feat = lambda rows: pl.BlockSpec((None, rows, tm), lambda i, j: (i, 0, j))
    return pl.pallas_call(
        functools.partial(_ffn_kernel, tm=tm, final_norm=final_norm),
        grid=(b, s // tm),
        in_specs=[
            tok(D_MODEL), tok(W_LRU), feat(SB_W), feat(DF_W),
            _resident((MIX, D_MODEL)),
            _resident((1, D_MODEL)),
            _resident((D_MODEL, 2 * D_FF)),
            _resident((CONV_FF, 2 * D_FF)),
            _resident((1, 2 * D_FF)),
            _resident((D_FF, D_MODEL)),
            _resident((1, D_MODEL)),
        ],
        out_specs=pl.BlockSpec((None, tm, D_MODEL), lambda i, j: (i, j, 0)),
        out_shape=jax.ShapeDtypeStruct((b, s, D_MODEL), F32),
        scratch_shapes=[
            pltpu.VMEM((SUBLANES + tm, FF_CHUNK), F32),
            pltpu.VMEM((2 * FF_NCHUNK, SUBLANES, FF_CHUNK), F32),
            pltpu.VMEM((tm, D_MODEL), F32),
        ],
        compiler_params=_params(("parallel", "arbitrary")),
        name="convffn",
    )(x3, oa, obt, oct, wo_bf, g, wu_bf, cw, cb, wd_bf, fg)


def _alibi_bias_table(s):
    resid = (np.asarray(DF_SLOPES, np.float32)[:, None] * np.float32(LOG2E)
             * np.arange(s, dtype=np.float32)[None, :]).astype(np.float32)
    terms = []
    for _ in range(DF_NBIAS):
        top = (resid.view(np.uint32) & np.uint32(0xFFFF0000)).view(np.float32)
        terms.append(top)
        resid = resid - top
    assert not resid.any()
    table = np.zeros((s, DF_HEADS, DF_KW), np.float32)
    table[:, :, 2 * DF_DQK:2 * DF_DQK + DF_NBIAS] = np.stack(terms, axis=-1).transpose(1, 0, 2)
    return jnp.asarray(table.reshape(s, DF_HEADS * DF_KW)).astype(BF16)


def _block_diag(w):
    n, bw, _ = w.shape
    eye = jnp.eye(n, dtype=w.dtype)
    return jnp.einsum("ncd,nm->ncmd", w, eye).reshape(n * bw, n * bw)


def kernel(x, norm1_g, w_in, conv_a_w, conv_a_b, w_rgate, b_rgate, w_igate, b_igate, lru_lambda, lam_q1, lam_k1, lam_q2, lam_k2, subln_g, w_out, norm2_g, w_ff_up, conv_ff_w, conv_ff_b, w_ff_down, final_g):
    s = x.shape[1]
    bias_tab = _alibi_bias_table(s)
    for l in range(DEPTH):
        lam_init = 0.8 - 0.6 * float(np.exp(-0.3 * l))
        xy, dk, skp, sqt, dqt, dvt, svt, kn = _inproj(x, norm1_g[l][None, :], w_in[l], bias_tab)

        wg = jnp.concatenate([_block_diag(w_rgate[l]), _block_diag(w_igate[l])], axis=1).astype(BF16)
        bg = jnp.concatenate([b_rgate[l], b_igate[l]])[None, :]
        out_a = _lru(xy, conv_a_w[l], conv_a_b[l][None, :], wg, bg, lru_lambda[l][None, :])
        out_bt = _stick_breaking(sqt, skp, svt)
        out_ct = _diff_attention(dqt, dk, dvt, kn[:, :, 0, :], lam_q1[l][None, :], lam_k1[l][None, :],
                                 lam_q2[l][None, :], lam_k2[l][None, :], subln_g[l][:, None], lam_init)

        x = _ffn(x, out_a, out_bt, out_ct, w_out[l].astype(BF16), norm2_g[l][None, :],
                 w_ff_up[l].astype(BF16), conv_ff_w[l], conv_ff_b[l][None, :],
                 w_ff_down[l].astype(BF16), final_g[None, :], l == DEPTH - 1)
    return x
```

```python
import functools

import jax
import jax.numpy as jnp
import numpy as np
from jax import lax
from jax.experimental import pallas as pl
from jax.experimental.pallas import tpu as pltpu

D_MODEL = 1024
BATCH = 4
SEQ = 8192
DEPTH = 2
CHUNK = 64
MIX = D_MODEL
W_LRU = MIX // 2
LRU_BLOCKS = 8
LRU_BW = W_LRU // LRU_BLOCKS
LRU_C = 8.0
CONV_A = 4
SB_HEADS = 4
SB_DH = MIX // 4 // SB_HEADS
SB_W = SB_HEADS * SB_DH
DF_HEADS = 4
DF_DV = MIX // 4 // DF_HEADS
DF_DQK = DF_DV // 2
DF_W = DF_HEADS * DF_DV
D_FF = ((8 * D_MODEL // 3 + 255) // 256) * 256
CONV_FF = 3
EPS = 1e-6
P_IN = 2 * W_LRU + 3 * SB_W + 3 * DF_W

SUBLANES = 8
VMEM_LIMIT = 56 * 1024 * 1024

ATT_BLK = 256
SEG = ATT_BLK // SUBLANES
NEG_BIG = -1e30
LOG2E = 1.4426950408889634

F32 = jnp.float32
BF16 = jnp.bfloat16


def _params(sem, vmem=VMEM_LIMIT):
    return pltpu.CompilerParams(dimension_semantics=sem, vmem_limit_bytes=vmem)


def _resident(shape):
    nd = len(shape)
    return pl.BlockSpec(shape, lambda *_: (0,) * nd, pipeline_mode=pl.Buffered(1))


def _rms(x, g):
    return x * lax.rsqrt(jnp.mean(x * x, axis=-1, keepdims=True) + EPS) * g


DF_KW = 128
DF_NBIAS = 3
DF_VR = 80


def _nt_dot(w_t, h):
    return lax.dot_general(w_t, h, (((1,), (1,)), ((), ())), preferred_element_type=F32)


def _inproj_kernel(x_ref, g_ref, w_tok_ref, w_sk_ref, wt_ref, wt_sv_ref, bias_ref, ones_ref, perm_ref,
                   sel_ref, xy_ref, dk_ref, skp_ref, sqt_ref, dqt_ref, dvt_ref, svt_ref, kn_ref, *, tm):
    h = _rms(x_ref[...], g_ref[...]).astype(BF16)
    perm = perm_ref[...]
    xy_ref[...] = jnp.dot(h, w_tok_ref[:, :2 * W_LRU], preferred_element_type=F32)
    dk = jnp.dot(h, w_tok_ref[:, 2 * W_LRU:], preferred_element_type=F32).astype(BF16)
    dk_ref[...] = dk + bias_ref[...]
    dkf = dk.astype(F32)
    norms = jnp.dot((dkf * dkf).astype(BF16), sel_ref[...], preferred_element_type=F32)
    kn_ref[...] = jnp.broadcast_to(jnp.max(norms, axis=0, keepdims=True), kn_ref.shape)
    sk = jnp.dot(h, w_sk_ref[...], preferred_element_type=F32).astype(BF16)
    svt = _nt_dot(wt_sv_ref[...], h).astype(BF16)
    for blk in range(tm // ATT_BLK):
        lo, hi = blk * ATT_BLK, (blk + 1) * ATT_BLK
        skp_ref[lo:hi, :] = jnp.dot(perm, sk[lo:hi, :], preferred_element_type=F32).astype(BF16)
        svt_ref[:, lo:hi] = _nt_dot(svt[:, lo:hi], perm).astype(BF16)
    sqt_ref[...] = (_nt_dot(wt_ref[:SB_W, :], h) * (-(SB_DH ** -0.5) * LOG2E)).astype(BF16)
    dqt_ref[...] = (_nt_dot(wt_ref[SB_W:SB_W + DF_W, :], h) * (DF_DQK ** -0.5 * LOG2E)).astype(BF16)
    dvt_ref[...] = (_nt_dot(wt_ref[SB_W + DF_W:, :], h) + ones_ref[...]).astype(BF16)


def _inproj(x, g, w, bias_tab, tm=512):
    b, s, d = x.shape
    o = 2 * W_LRU
    w_sq, w_sk, w_sv = (w[:, o + k * SB_W:o + (k + 1) * SB_W] for k in range(3))
    o += 3 * SB_W
    w_dq, w_dk, w_dv = (w[:, o + k * DF_W:o + (k + 1) * DF_W] for k in range(3))
    dk_pad = jnp.pad(w_dk.reshape(d, DF_HEADS, DF_DV), ((0, 0), (0, 0), (0, DF_KW - DF_DV)))
    dv_pad = jnp.pad(w_dv.T.reshape(DF_HEADS, DF_DV, d), ((0, 0), (0, DF_VR - DF_DV), (0, 0)))
    w_tok = jnp.concatenate([w[:, :2 * W_LRU], dk_pad.reshape(d, DF_HEADS * DF_KW)], axis=1).astype(BF16)
    wt = jnp.concatenate([w_sq.T, w_dq.T, dv_pad.reshape(DF_HEADS * DF_VR, d)], axis=0).astype(BF16)
    ones_col = np.zeros((DF_HEADS * DF_VR, 1), np.float32)
    ones_col[DF_DV::DF_VR] = 1.0
    new_row = np.arange(ATT_BLK)
    perm = np.zeros((ATT_BLK, ATT_BLK), np.float32)
    perm[new_row, (new_row % SUBLANES) * SEG + new_row // SUBLANES] = 1.0
    sel = np.zeros((DF_HEADS, DF_KW, DF_KW), np.float32)
    for h in range(DF_HEADS):
        for mp in range(2):
            sel[h, mp * DF_DQK:(mp + 1) * DF_DQK, 2 * h + mp] = 1.0
    sel = sel.reshape(DF_HEADS * DF_KW, DF_KW)
    assert tm == DF_QB
    tok = lambda width: pl.BlockSpec((None, tm, width), lambda i, j: (i, j, 0))
    feat = lambda rows: pl.BlockSpec((None, rows, tm), lambda i, j: (i, 0, j))
    return pl.pallas_call(
        functools.partial(_inproj_kernel, tm=tm),
        grid=(b, s // tm),
        in_specs=[
            tok(d),
            _resident((1, d)),
            _resident(w_tok.shape),
            _resident((d, SB_W)),
            _resident(wt.shape),
            _resident((SB_W, d)),
            pl.BlockSpec((tm, DF_HEADS * DF_KW), lambda i, j: (j, 0)),
            _resident(ones_col.shape),
            _resident(perm.shape),
            _resident(sel.shape),
        ],
        out_specs=[tok(2 * W_LRU), tok(DF_HEADS * DF_KW), tok(SB_W),
                   feat(SB_W), feat(DF_W), feat(DF_HEADS * DF_VR), feat(SB_W),
                   pl.BlockSpec((None, None, SUBLANES, DF_KW), lambda i, j: (i, j, 0, 0))],
        out_shape=[
            jax.ShapeDtypeStruct((b, s, 2 * W_LRU), F32),
            jax.ShapeDtypeStruct((b, s, DF_HEADS * DF_KW), BF16),
            jax.ShapeDtypeStruct((b, s, SB_W), BF16),
            jax.ShapeDtypeStruct((b, SB_W, s), BF16),
            jax.ShapeDtypeStruct((b, DF_W, s), BF16),
            jax.ShapeDtypeStruct((b, DF_HEADS * DF_VR, s), BF16),
            jax.ShapeDtypeStruct((b, SB_W, s), BF16),
            jax.ShapeDtypeStruct((b, s // tm, SUBLANES, DF_KW), F32),
        ],
        compiler_params=_params(("parallel", "parallel")),
        name="inproj",
    )(x, g, w_tok, w_sk.astype(BF16), wt, w_sv.T.astype(BF16), bias_tab, jnp.asarray(ones_col),
      jnp.asarray(perm).astype(BF16), jnp.asarray(sel).astype(BF16))


def _softplus(x):
    return jnp.maximum(x, 0.0) + jnp.log(1.0 + jnp.exp(-jnp.abs(x)))


def _sigmoid(x):
    return 1.0 / (1.0 + jnp.exp(-x))


def _lru_kernel(xa_ref, ya_ref, cw_ref, cb_ref, wg_ref, bg_ref, lam_ref, o_ref,
                xbuf, hcar, *, ts):
    s = pl.program_id(1)

    @pl.when(s == 0)
    def _():
        xbuf[0:SUBLANES, :] = jnp.zeros((SUBLANES, W_LRU), F32)
        hcar[...] = jnp.zeros_like(hcar)

    xbuf[SUBLANES:, :] = xa_ref[...]
    xc = cb_ref[...] + cw_ref[CONV_A - 1:CONV_A, :] * xa_ref[...]
    for k in range(1, CONV_A):
        xc = xc + cw_ref[CONV_A - 1 - k:CONV_A - k, :] * xbuf[pl.ds(SUBLANES - k, ts), :]
    xbuf[0:SUBLANES, :] = xa_ref[ts - SUBLANES:, :]

    gates = jnp.dot(xc.astype(BF16), wg_ref[...], preferred_element_type=F32) + bg_ref[...]
    r = _sigmoid(gates[:, :W_LRU])
    ig = _sigmoid(gates[:, W_LRU:])
    log_a = (-LRU_C) * r * _softplus(-lam_ref[...])
    a = jnp.exp(log_a)
    u = jnp.sqrt(1.0 - a * a) * (ig * xc)

    groups = ts // SUBLANES
    a = a.reshape(groups, SUBLANES, W_LRU)
    u = u.reshape(groups, SUBLANES, W_LRU)
    sub = lax.broadcasted_iota(jnp.int32, a.shape, 1)
    d = 1
    while d < SUBLANES:
        a_sh = jnp.where(sub >= d, pltpu.roll(a, d, axis=1), 1.0)
        u_sh = jnp.where(sub >= d, pltpu.roll(u, d, axis=1), 0.0)
        u = a * u_sh + u
        a = a * a_sh
        d *= 2
    carry = hcar[...]
    hs = []
    for grp in range(groups):
        hs.append(a[grp] * carry + u[grp])
        carry = hs[-1][SUBLANES - 1:SUBLANES, :]
    hcar[...] = carry
    o_ref[...] = (jnp.concatenate(hs, axis=0) * jax.nn.gelu(ya_ref[...])).astype(BF16)


def _lru(xy, cw, cb, wg_bf, bg, lam, ts=256):
    b, s, _ = xy.shape
    return pl.pallas_call(
        functools.partial(_lru_kernel, ts=ts),
        grid=(b, s // ts),
        in_specs=[
            pl.BlockSpec((None, ts, W_LRU), lambda i, j: (i, j, 0)),
            pl.BlockSpec((None, ts, W_LRU), lambda i, j: (i, j, 1)),
            _resident((CONV_A, W_LRU)),
            _resident((1, W_LRU)),
            _resident((W_LRU, 2 * W_LRU)),
            _resident((1, 2 * W_LRU)),
            _resident((1, W_LRU)),
        ],
        out_specs=pl.BlockSpec((None, ts, W_LRU), lambda i, j: (i, j, 0)),
        out_shape=jax.ShapeDtypeStruct((b, s, W_LRU), BF16),
        scratch_shapes=[
            pltpu.VMEM((SUBLANES + ts, W_LRU), F32),
            pltpu.VMEM((1, W_LRU), F32),
        ],
        compiler_params=_params(("parallel", "arbitrary")),
        name="rglru",
    )(xy, xy, cw, cb, wg_bf, bg, lam)


def _perm_local_key():
    row = lax.broadcasted_iota(jnp.int32, (ATT_BLK, ATT_BLK), 0)
    return (row & (SUBLANES - 1)) * SEG + (row >> 3)


def _sb_block(zneg, run, diag):
    beta = 1.0 / (1.0 + jnp.exp2(zneg))
    om = 1.0 - beta
    if diag:
        earlier = _perm_local_key() < lax.broadcasted_iota(jnp.int32, zneg.shape, 1)
        om = jnp.where(earlier, om, 1.0)
    om_r = [om[r * SUBLANES:(r + 1) * SUBLANES, :] for r in range(SEG)]
    seg_tot = om_r[0]
    for r in range(1, SEG):
        seg_tot = seg_tot * om_r[r]
    sub = lax.broadcasted_iota(jnp.int32, seg_tot.shape, 0)
    tail = jnp.broadcast_to(run, seg_tot.shape)
    for u in range(1, SUBLANES):
        tail = tail * jnp.where(sub < u, seg_tot[u:u + 1, :], 1.0)
    ws = [None] * SEG
    for r in reversed(range(SEG)):
        ws[r] = beta[r * SUBLANES:(r + 1) * SUBLANES, :] * tail
        tail = tail * om_r[r]
    w = jnp.concatenate(ws, axis=0)
    if diag:
        w = jnp.where(earlier, w, 0.0)
    return w.astype(BF16), tail[0:1, :]


def _sb_kernel(qt_ref, k_ref, vt_ref, o_ref):
    i = pl.program_id(1)
    q_all = qt_ref[...].astype(F32)
    head_of_row = lax.broadcasted_iota(jnp.int32, q_all.shape, 0) >> 6
    qs = [jnp.where(head_of_row == h, q_all, 0.0).astype(BF16) for h in range(SB_HEADS)]

    def step(blocks, runs, accs):
        offs = [pl.multiple_of(j * ATT_BLK, ATT_BLK) for j, _, _ in blocks]
        zs = [[jnp.dot(k_ref[pl.ds(off, ATT_BLK), :], qs[h], preferred_element_type=F32)
               for h in range(SB_HEADS)] for off in offs]
        runs, accs = list(runs), list(accs)
        for (_, diag, live), off, z in zip(blocks, offs, zs):
            for h in range(SB_HEADS):
                w, run = _sb_block(z[h], runs[h], diag)
                if live is not None:
                    w = jnp.where(live, w, jnp.zeros_like(w))
                    run = jnp.where(live, run, runs[h])
                runs[h] = run
                vj = vt_ref[h * SB_DH:(h + 1) * SB_DH, pl.ds(off, ATT_BLK)]
                accs[h] = accs[h] + jnp.dot(vj, w, preferred_element_type=F32)
        return tuple(runs), tuple(accs)

    def largest(runs):
        top = runs[0]
        for r in runs[1:]:
            top = jnp.maximum(top, r)
        return jnp.max(top)

    runs = tuple(jnp.ones((1, ATT_BLK), F32) for _ in range(SB_HEADS))
    accs = tuple(jnp.zeros((SB_DH, ATT_BLK), F32) for _ in range(SB_HEADS))
    runs, accs = step([(i, True, None), (jnp.maximum(i - 1, 0), False, i > 0)], runs, accs)

    def cond(c):
        return jnp.logical_and(c[0] >= 0, c[1] > 0.0)

    def body(c):
        j, _, runs, accs = c
        runs, accs = step([(j, False, None)], runs, accs)
        return j - 1, largest(runs), runs, accs

    _, _, _, accs = lax.while_loop(cond, body, (i - 2, largest(runs), runs, accs))
    for h in range(SB_HEADS):
        o_ref[h * SB_DH:(h + 1) * SB_DH, :] = accs[h].astype(o_ref.dtype)


def _stick_breaking(qt, kp, vt):
    b, w, s = qt.shape
    return pl.pallas_call(
        _sb_kernel,
        grid=(b, s // ATT_BLK),
        in_specs=[
            pl.BlockSpec((None, w, ATT_BLK), lambda bi, i: (bi, 0, i)),
            pl.BlockSpec((None, s, w), lambda bi, i: (bi, 0, 0)),
            pl.BlockSpec((None, w, s), lambda bi, i: (bi, 0, 0)),
        ],
        out_specs=pl.BlockSpec((None, w, ATT_BLK), lambda bi, i: (bi, 0, i)),
        out_shape=jax.ShapeDtypeStruct((b, w, s), BF16),
        compiler_params=_params(("parallel", "arbitrary")),
        name="stickbreak",
    )(qt, kp, vt)


DF_SLOPES = tuple(2.0 ** (-8.0 * (h + 1) / DF_HEADS) for h in range(DF_HEADS))
DF_QB = 512
DF_KB = 256
DF_KPQ = DF_QB // DF_KB
DF_SKIP = 150.0
DF_GROUPS = (((0, 1), 1), ((2, 3), 2))


def _df_kernel(q_ref, k_ref, vt_ref, kn_ref, lq1, lk1, lq2, lk2, gs_ref, o_ref, acc_ref, kp_ref,
               *, lam_init):
    i = pl.program_id(1)
    acc_ref[...] = jnp.zeros_like(acc_ref)
    zero = jnp.zeros((DF_DQK, DF_QB), BF16)
    rest = lax.broadcasted_iota(jnp.int32, (DF_KW - 2 * DF_DQK, DF_QB), 0)
    pick_bias = jnp.where(rest < DF_NBIAS, 1.0, 0.0).astype(BF16)
    qa, qb = [], []
    for h in range(DF_HEADS):
        q1 = q_ref[h * DF_DV:h * DF_DV + DF_DQK, :]
        q2 = q_ref[h * DF_DV + DF_DQK:(h + 1) * DF_DV, :]
        qa.append(jnp.concatenate([q1, zero, pick_bias], axis=0))
        qb.append(jnp.concatenate([zero, q2, pick_bias], axis=0))

    def update(s_, m, acc, vj):
        m_new = jnp.maximum(m, jnp.max(s_, axis=0, keepdims=True))
        p = jnp.exp2(s_ - m_new).astype(BF16)
        acc[...] = jnp.exp2(m - m_new) * acc[...] + jnp.dot(vj, p, preferred_element_type=F32)
        return m_new

    def score_dots(offs, heads, in_tile):
        out = []
        for d, off in enumerate(offs):
            lo = d * DF_KB if in_tile else 0
            for h in heads:
                kj = k_ref[pl.ds(off, DF_KB), h * DF_KW:(h + 1) * DF_KW]
                q1, q2 = (qa[h], qb[h]) if lo == 0 else (qa[h][:, lo:], qb[h][:, lo:])
                out.append(jnp.dot(kj, q1, preferred_element_type=F32))
                out.append(jnp.dot(kj, q2, preferred_element_type=F32))
        return out

    def step(tiles, ms, heads, in_tile):
        offs = [pl.multiple_of((jt * DF_KPQ + d) * DF_KB, DF_KB) for jt in tiles for d in range(DF_KPQ)]
        scores = score_dots(offs, heads, in_tile)
        ms = list(ms)
        for d, off in enumerate(offs):
            lo = d * DF_KB if in_tile else 0
            if in_tile:
                kpos = lax.broadcasted_iota(jnp.int32, (DF_KB, DF_QB - lo), 0) + d * DF_KB
                qpos = lax.broadcasted_iota(jnp.int32, (DF_KB, DF_QB - lo), 1) + lo
                ahead = jnp.maximum(kpos - qpos, 0).astype(F32)
                allowed = (kpos >> 6) <= (qpos >> 6)
            for n, h in enumerate(heads):
                vj = vt_ref[h * DF_VR:(h + 1) * DF_VR, pl.ds(off, DF_KB)]
                new = []
                for mp in range(2):
                    s_ = scores[(d * len(heads) + n) * 2 + mp]
                    if in_tile:
                        s_ = jnp.where(allowed, s_ - (2.0 * DF_SLOPES[h] * LOG2E) * ahead, NEG_BIG)
                    m_old = ms[n][mp]
                    if lo == 0:
                        new.append(update(s_, m_old, acc_ref.at[h, mp], vj))
                    else:
                        part = update(s_, m_old[:, lo:], acc_ref.at[h, mp, :, pl.ds(lo, DF_QB - lo)], vj)
                        new.append(jnp.concatenate([m_old[:, :lo], part], axis=1))
                ms[n] = tuple(new)
        return tuple(ms)

    lane = lax.broadcasted_iota(jnp.int32, (1, DF_KW), 1)

    def per_map(values):
        out = jnp.zeros((1, DF_KW), F32)
        for h, pair in values.items():
            for mp, v in enumerate(pair):
                out = jnp.where(lane == 2 * h + mp, v, out)
        return out

    def sq_norm(q):
        sq = q.astype(F32)
        return jnp.max(jnp.sum(sq * sq, axis=0, keepdims=True), axis=1, keepdims=True)

    qn2 = per_map({h: (sq_norm(qa[h][:DF_DQK, :]), sq_norm(qb[h][DF_DQK:2 * DF_DQK, :]))
                   for h in range(DF_HEADS)})
    slope2 = per_map({h: (jnp.full((1, 1), DF_SLOPES[h] * LOG2E, F32),) * 2 for h in range(DF_HEADS)})
    top = kn_ref[0:1, :]
    kp_rows = [top]
    for r in range(1, kn_ref.shape[0]):
        top = jnp.maximum(top, kn_ref[r:r + 1, :])
        kp_rows.append(top)
    kp_ref[...] = jnp.concatenate(kp_rows, axis=0)

    def needed(jt, ms, heads):
        jt = jnp.maximum(jt, 0)
        low = per_map({h: tuple(jnp.min(m, axis=1, keepdims=True) for m in ms[n])
                       for n, h in enumerate(heads)})
        last_key = (jnp.full((1, DF_KW), jt, jnp.int32) * DF_QB + (DF_QB - 1)).astype(F32)
        room = low - (DF_SKIP + 1.0) - slope2 * last_key
        reach = jnp.logical_or(room <= 0.0, qn2 * kp_ref[pl.ds(jt, 1), :] * 1.02 >= room * room)
        mine = jnp.logical_and(lane >= 2 * heads[0], lane < 2 * heads[-1] + 2)
        return jnp.max(jnp.where(jnp.logical_and(mine, reach), 1.0, 0.0))

    m0 = jnp.full((1, DF_QB), NEG_BIG, F32)
    all_heads = tuple(range(DF_HEADS))
    ms_all = step([i], tuple((m0, m0) for _ in all_heads), all_heads, True)
    for heads, span in DF_GROUPS:
        ms = tuple(ms_all[h] for h in heads)
        for r in range(span - 1):
            top = i - 1 - r
            todo = jnp.logical_and(lax.rem(i, span) > r, needed(top, ms, heads) > 0.5)
            ms = lax.cond(todo, lambda ms, top=top, heads=heads: step([top], ms, heads, False),
                          lambda ms: ms, ms)

        def cond(c):
            return jnp.logical_and(c[0] >= 0, c[1] > 0.5)

        def body(c, heads=heads, span=span):
            g, _, ms = c
            ms = step([g * span + r for r in reversed(range(span))], ms, heads, False)
            return g - 1, needed(g * span - 1, ms, heads), ms

        groups = i // span
        lax.while_loop(cond, body, (groups - 1, needed(groups * span - 1, ms, heads), ms))

    lam = (jnp.exp(jnp.sum(lq1[...] * lk1[...], axis=-1, keepdims=True))
           - jnp.exp(jnp.sum(lq2[...] * lk2[...], axis=-1, keepdims=True)) + lam_init)
    for h in range(DF_HEADS):
        a1 = acc_ref[h, 0]
        a2 = acc_ref[h, 1]
        o = a1[:DF_DV, :] / a1[DF_DV:DF_DV + 1, :] - lam * (a2[:DF_DV, :] / a2[DF_DV:DF_DV + 1, :])
        o = o * lax.rsqrt(jnp.mean(o * o, axis=0, keepdims=True) + EPS) * gs_ref[...]
        o_ref[h * DF_DV:(h + 1) * DF_DV, :] = (o * (1.0 - lam_init)).astype(o_ref.dtype)


def _diff_attention(qt, kall, vt, kn, lq1, lk1, lq2, lk2, gs, lam_init):
    b, w, s = qt.shape
    vec = _resident((1, DF_DQK))
    return pl.pallas_call(
        functools.partial(_df_kernel, lam_init=lam_init),
        grid=(b, s // DF_QB),
        in_specs=[
            pl.BlockSpec((None, w, DF_QB), lambda bi, i: (bi, 0, i)),
            pl.BlockSpec((None, s, DF_HEADS * DF_KW), lambda bi, i: (bi, 0, 0)),
            pl.BlockSpec((None, DF_HEADS * DF_VR, s), lambda bi, i: (bi, 0, 0)),
            pl.BlockSpec((None, s // DF_QB, DF_KW), lambda bi, i: (bi, 0, 0)),
            vec, vec, vec, vec,
            _resident((DF_DV, 1)),
        ],
        out_specs=pl.BlockSpec((None, w, DF_QB), lambda bi, i: (bi, 0, i)),
        out_shape=jax.ShapeDtypeStruct((b, w, s), BF16),
        scratch_shapes=[pltpu.VMEM((DF_HEADS, 2, DF_VR, DF_QB), F32),
                        pltpu.VMEM((s // DF_QB, DF_KW), F32)],
        compiler_params=_params(("parallel", "arbitrary")),
        name="diffattn",
    )(qt, kall, vt, kn, lq1, lk1, lq2, lk2, gs)


FF_CHUNK = 256
FF_NCHUNK = D_FF // FF_CHUNK


def _tn_dot(a_t, w):
    return lax.dot_general(a_t, w, (((0,), (0,)), ((), ())), preferred_element_type=F32)


def _ffn_kernel(x_ref, a_ref, bt_ref, ct_ref, wo_ref, g_ref, wu_ref, cw_ref, cb_ref, wd_ref, fg_ref,
                o_ref, ubuf, tails, acc_ref, *, tm, final_norm):
    s = pl.program_id(1)

    @pl.when(s == 0)
    def _():
        tails[...] = jnp.zeros_like(tails)

    mix = jnp.dot(a_ref[...], wo_ref[:W_LRU, :], preferred_element_type=F32)
    mix = mix + _tn_dot(bt_ref[...], wo_ref[W_LRU:W_LRU + SB_W, :])
    mix = mix + _tn_dot(ct_ref[...], wo_ref[W_LRU + SB_W:, :])
    x = x_ref[...] + mix
    h = _rms(x, g_ref[...]).astype(BF16)

    def up(c):
        return tuple(jnp.dot(h, wu_ref[:, pl.ds(col, FF_CHUNK)], preferred_element_type=F32)
                     for col in (c * FF_CHUNK, D_FF + c * FF_CHUNK))

    def conv(c, col, u):
        ubuf[0:SUBLANES, :] = tails[c]
        ubuf[SUBLANES:, :] = u
        tails[c] = u[tm - SUBLANES:, :]
        w = cw_ref[:, pl.ds(col, FF_CHUNK)]
        y = cb_ref[:, pl.ds(col, FF_CHUNK)] + w[CONV_FF - 1:CONV_FF, :] * u
        for k in range(1, CONV_FF):
            y = y + w[CONV_FF - 1 - k:CONV_FF - k, :] * ubuf[pl.ds(SUBLANES - k, tm), :]
        return y

    nxt = up(0)
    for c in range(FF_NCHUNK):
        ug, uv = nxt
        if c + 1 < FF_NCHUNK:
            nxt = up(c + 1)
        gate = conv(2 * c, c * FF_CHUNK, ug)
        val = conv(2 * c + 1, D_FF + c * FF_CHUNK, uv)
        act = (gate * _sigmoid(gate) * val).astype(BF16)
        part = jnp.dot(act, wd_ref[c * FF_CHUNK:(c + 1) * FF_CHUNK, :], preferred_element_type=F32)
        if c == 0:
            acc_ref[...] = part
        else:
            acc_ref[...] += part
    y = x + acc_ref[...]
    if final_norm:
        y = _rms(y, fg_ref[...])
    o_ref[...] = y


def _ffn(x3, oa, obt, oct, wo_bf, g, wu_bf, cw, cb, wd_bf, fg, final_norm, tm=512):
    b, s, _ = x3.shape
    tok = lambda width: pl.BlockSpec((None, tm, width), lambda i, j: (i, j, 0))
    feat = lambda rows: pl.BlockSpec((None, rows, tm), lambda i, j: (i, 0, j))
    return pl.pallas_call(
        functools.partial(_ffn_kernel, tm=tm, final_norm=final_norm),
        grid=(b, s // tm),
        in_specs=[
            tok(D_MODEL), tok(W_LRU), feat(SB_W), feat(DF_W),
            _resident((MIX, D_MODEL)),
            _resident((1, D_MODEL)),
            _resident((D_MODEL, 2 * D_FF)),
            _resident((CONV_FF, 2 * D_FF)),
            _resident((1, 2 * D_FF)),
            _resident((D_FF, D_MODEL)),
            _resident((1, D_MODEL)),
        ],
        out_specs=pl.BlockSpec((None, tm, D_MODEL), lambda i, j: (i, j, 0)),
        out_shape=jax.ShapeDtypeStruct((b, s, D_MODEL), F32),
        scratch_shapes=[
            pltpu.VMEM((SUBLANES + tm, FF_CHUNK), F32),
            pltpu.VMEM((2 * FF_NCHUNK, SUBLANES, FF_CHUNK), F32),
            pltpu.VMEM((tm, D_MODEL), F32),
        ],
        compiler_params=_params(("parallel", "arbitrary")),
        name="convffn",
    )(x3, oa, obt, oct, wo_bf, g, wu_bf, cw, cb, wd_bf, fg)


def _alibi_bias_table(s):
    resid = (np.asarray(DF_SLOPES, np.float32)[:, None] * np.float32(LOG2E)
             * np.arange(s, dtype=np.float32)[None, :]).astype(np.float32)
    terms = []
    for _ in range(DF_NBIAS):
        top = (resid.view(np.uint32) & np.uint32(0xFFFF0000)).view(np.float32)
        terms.append(top)
        resid = resid - top
    assert not resid.any()
    table = np.zeros((s, DF_HEADS, DF_KW), np.float32)
    table[:, :, 2 * DF_DQK:2 * DF_DQK + DF_NBIAS] = np.stack(terms, axis=-1).transpose(1, 0, 2)
    return jnp.asarray(table.reshape(s, DF_HEADS * DF_KW)).astype(BF16)


def _block_diag(w):
    n, bw, _ = w.shape
    eye = jnp.eye(n, dtype=w.dtype)
    return jnp.einsum("ncd,nm->ncmd", w, eye).reshape(n * bw, n * bw)


def kernel(x, norm1_g, w_in, conv_a_w, conv_a_b, w_rgate, b_rgate, w_igate, b_igate, lru_lambda, lam_q1, lam_k1, lam_q2, lam_k2, subln_g, w_out, norm2_g, w_ff_up, conv_ff_w, conv_ff_b, w_ff_down, final_g):
    s = x.shape[1]
    bias_tab = _alibi_bias_table(s)
    for l in range(DEPTH):
        lam_init = 0.8 - 0.6 * float(np.exp(-0.3 * l))
        xy, dk, skp, sqt, dqt, dvt, svt, kn = _inproj(x, norm1_g[l][None, :], w_in[l], bias_tab)

        wg = jnp.concatenate([_block_diag(w_rgate[l]), _block_diag(w_igate[l])], axis=1).astype(BF16)
        bg = jnp.concatenate([b_rgate[l], b_igate[l]])[None, :]
        out_a = _lru(xy, conv_a_w[l], conv_a_b[l][None, :], wg, bg, lru_lambda[l][None, :])
        out_bt = _stick_breaking(sqt, skp, svt)
        out_ct = _diff_attention(dqt, dk, dvt, kn[:, :, 0, :], lam_q1[l][None, :], lam_k1[l][None, :],
                                 lam_q2[l][None, :], lam_k2[l][None, :], subln_g[l][:, None], lam_init)

        x = _ffn(x, out_a, out_bt, out_ct, w_out[l].astype(BF16), norm2_g[l][None, :],
                 w_ff_up[l].astype(BF16), conv_ff_w[l], conv_ff_b[l][None, :],
                 w_ff_down[l].astype(BF16), final_g[None, :], l == DEPTH - 1)
    return x
```

```python
import functools

import jax
import jax.numpy as jnp
import numpy as np
from jax import lax
from jax.experimental import pallas as pl
from jax.experimental.pallas import tpu as pltpu

D_MODEL = 1024
BATCH = 4
SEQ = 8192
DEPTH = 2
CHUNK = 64
MIX = D_MODEL
W_LRU = MIX // 2
LRU_BLOCKS = 8
LRU_BW = W_LRU // LRU_BLOCKS
LRU_C = 8.0
CONV_A = 4
SB_HEADS = 4
SB_DH = MIX // 4 // SB_HEADS
SB_W = SB_HEADS * SB_DH
DF_HEADS = 4
DF_DV = MIX // 4 // DF_HEADS
DF_DQK = DF_DV // 2
DF_W = DF_HEADS * DF_DV
D_FF = ((8 * D_MODEL // 3 + 255) // 256) * 256
CONV_FF = 3
EPS = 1e-6
P_IN = 2 * W_LRU + 3 * SB_W + 3 * DF_W

SUBLANES = 8
VMEM_LIMIT = 56 * 1024 * 1024

ATT_BLK = 256
SEG = ATT_BLK // SUBLANES
NEG_BIG = -1e30
LOG2E = 1.4426950408889634

F32 = jnp.float32
BF16 = jnp.bfloat16


def _params(sem, vmem=VMEM_LIMIT):
    return pltpu.CompilerParams(dimension_semantics=sem, vmem_limit_bytes=vmem)


def _resident(shape):
    nd = len(shape)
    return pl.BlockSpec(shape, lambda *_: (0,) * nd, pipeline_mode=pl.Buffered(1))


def _rms(x, g):
    return x * lax.rsqrt(jnp.mean(x * x, axis=-1, keepdims=True) + EPS) * g


DF_KW = 128
DF_NBIAS = 3
DF_VR = 80


def _nt_dot(w_t, h):
    return lax.dot_general(w_t, h, (((1,), (1,)), ((), ())), preferred_element_type=F32)


def _inproj_kernel(x_ref, g_ref, w_tok_ref, w_sk_ref, wt_ref, wt_sv_ref, bias_ref, ones_ref, perm_ref,
                   sel_ref, xy_ref, dk_ref, skp_ref, sqt_ref, dqt_ref, dvt_ref, svt_ref, kn_ref, *, tm):
    h = _rms(x_ref[...], g_ref[...]).astype(BF16)
    perm = perm_ref[...]
    xy_ref[...] = jnp.dot(h, w_tok_ref[:, :2 * W_LRU], preferred_element_type=F32)
    dk = jnp.dot(h, w_tok_ref[:, 2 * W_LRU:], preferred_element_type=F32).astype(BF16)
    dk_ref[...] = dk + bias_ref[...]
    dkf = dk.astype(F32)
    norms = jnp.dot((dkf * dkf).astype(BF16), sel_ref[...], preferred_element_type=F32)
    kn_ref[...] = jnp.broadcast_to(jnp.max(norms, axis=0, keepdims=True), kn_ref.shape)
    sk = jnp.dot(h, w_sk_ref[...], preferred_element_type=F32).astype(BF16)
    svt = _nt_dot(wt_sv_ref[...], h).astype(BF16)
    for blk in range(tm // ATT_BLK):
        lo, hi = blk * ATT_BLK, (blk + 1) * ATT_BLK
        skp_ref[lo:hi, :] = jnp.dot(perm, sk[lo:hi, :], preferred_element_type=F32).astype(BF16)
        svt_ref[:, lo:hi] = _nt_dot(svt[:, lo:hi], perm).astype(BF16)
    sqt_ref[...] = (_nt_dot(wt_ref[:SB_W, :], h) * (-(SB_DH ** -0.5) * LOG2E)).astype(BF16)
    dqt_ref[...] = (_nt_dot(wt_ref[SB_W:SB_W + DF_W, :], h) * (DF_DQK ** -0.5 * LOG2E)).astype(BF16)
    dvt_ref[...] = (_nt_dot(wt_ref[SB_W + DF_W:, :], h) + ones_ref[...]).astype(BF16)


def _inproj(x, g, w, bias_tab, tm=512):
    b, s, d = x.shape
    o = 2 * W_LRU
    w_sq, w_sk, w_sv = (w[:, o + k * SB_W:o + (k + 1) * SB_W] for k in range(3))
    o += 3 * SB_W
    w_dq, w_dk, w_dv = (w[:, o + k * DF_W:o + (k + 1) * DF_W] for k in range(3))
    dk_pad = jnp.pad(w_dk.reshape(d, DF_HEADS, DF_DV), ((0, 0), (0, 0), (0, DF_KW - DF_DV)))
    dv_pad = jnp.pad(w_dv.T.reshape(DF_HEADS, DF_DV, d), ((0, 0), (0, DF_VR - DF_DV), (0, 0)))
    w_tok = jnp.concatenate([w[:, :2 * W_LRU], dk_pad.reshape(d, DF_HEADS * DF_KW)], axis=1).astype(BF16)
    wt = jnp.concatenate([w_sq.T, w_dq.T, dv_pad.reshape(DF_HEADS * DF_VR, d)], axis=0).astype(BF16)
    ones_col = np.zeros((DF_HEADS * DF_VR, 1), np.float32)
    ones_col[DF_DV::DF_VR] = 1.0
    new_row = np.arange(ATT_BLK)
    perm = np.zeros((ATT_BLK, ATT_BLK), np.float32)
    perm[new_row, (new_row % SUBLANES) * SEG + new_row // SUBLANES] = 1.0
    sel = np.zeros((DF_HEADS, DF_KW, DF_KW), np.float32)
    for h in range(DF_HEADS):
        for mp in range(2):
            sel[h, mp * DF_DQK:(mp + 1) * DF_DQK, 2 * h + mp] = 1.0
    sel = sel.reshape(DF_HEADS * DF_KW, DF_KW)
    assert tm == DF_QB
    tok = lambda width: pl.BlockSpec((None, tm, width), lambda i, j: (i, j, 0))
    feat = lambda rows: pl.BlockSpec((None, rows, tm), lambda i, j: (i, 0, j))
    return pl.pallas_call(
        functools.partial(_inproj_kernel, tm=tm),
        grid=(b, s // tm),
        in_specs=[
            tok(d),
            _resident((1, d)),
            _resident(w_tok.shape),
            _resident((d, SB_W)),
            _resident(wt.shape),
            _resident((SB_W, d)),
            pl.BlockSpec((tm, DF_HEADS * DF_KW), lambda i, j: (j, 0)),
            _resident(ones_col.shape),
            _resident(perm.shape),
            _resident(sel.shape),
        ],
        out_specs=[tok(2 * W_LRU), tok(DF_HEADS * DF_KW), tok(SB_W),
                   feat(SB_W), feat(DF_W), feat(DF_HEADS * DF_VR), feat(SB_W),
                   pl.BlockSpec((None, None, SUBLANES, DF_KW), lambda i, j: (i, j, 0, 0))],
        out_shape=[
            jax.ShapeDtypeStruct((b, s, 2 * W_LRU), F32),
            jax.ShapeDtypeStruct((b, s, DF_HEADS * DF_KW), BF16),
            jax.ShapeDtypeStruct((b, s, SB_W), BF16),
            jax.ShapeDtypeStruct((b, SB_W, s), BF16),
            jax.ShapeDtypeStruct((b, DF_W, s), BF16),
            jax.ShapeDtypeStruct((b, DF_HEADS * DF_VR, s), BF16),
            jax.ShapeDtypeStruct((b, SB_W, s), BF16),
            jax.ShapeDtypeStruct((b, s // tm, SUBLANES, DF_KW), F32),
        ],
        compiler_params=_params(("parallel", "parallel")),
        name="inproj",
    )(x, g, w_tok, w_sk.astype(BF16), wt, w_sv.T.astype(BF16), bias_tab, jnp.asarray(ones_col),
      jnp.asarray(perm).astype(BF16), jnp.asarray(sel).astype(BF16))


def _softplus(x):
    return jnp.maximum(x, 0.0) + jnp.log(1.0 + jnp.exp(-jnp.abs(x)))


def _sigmoid(x):
    return 1.0 / (1.0 + jnp.exp(-x))


def _lru_kernel(xa_ref, ya_ref, cw_ref, cb_ref, wg_ref, bg_ref, lam_ref, o_ref,
                xbuf, hcar, *, ts):
    s = pl.program_id(1)

    @pl.when(s == 0)
    def _():
        xbuf[0:SUBLANES, :] = jnp.zeros((SUBLANES, W_LRU), F32)
        hcar[...] = jnp.zeros_like(hcar)

    xbuf[SUBLANES:, :] = xa_ref[...]
    xc = cb_ref[...] + cw_ref[CONV_A - 1:CONV_A, :] * xa_ref[...]
    for k in range(1, CONV_A):
        xc = xc + cw_ref[CONV_A - 1 - k:CONV_A - k, :] * xbuf[pl.ds(SUBLANES - k, ts), :]
    xbuf[0:SUBLANES, :] = xa_ref[ts - SUBLANES:, :]

    gates = jnp.dot(xc.astype(BF16), wg_ref[...], preferred_element_type=F32) + bg_ref[...]
    r = _sigmoid(gates[:, :W_LRU])
    ig = _sigmoid(gates[:, W_LRU:])
    log_a = (-LRU_C) * r * _softplus(-lam_ref[...])
    a = jnp.exp(log_a)
    u = jnp.sqrt(1.0 - a * a) * (ig * xc)

    groups = ts // SUBLANES
    a = a.reshape(groups, SUBLANES, W_LRU)
    u = u.reshape(groups, SUBLANES, W_LRU)
    sub = lax.broadcasted_iota(jnp.int32, a.shape, 1)
    d = 1
    while d < SUBLANES:
        a_sh = jnp.where(sub >= d, pltpu.roll(a, d, axis=1), 1.0)
        u_sh = jnp.where(sub >= d, pltpu.roll(u, d, axis=1), 0.0)
        u = a * u_sh + u
        a = a * a_sh
        d *= 2
    carry = hcar[...]
    hs = []
    for grp in range(groups):
        hs.append(a[grp] * carry + u[grp])
        carry = hs[-1][SUBLANES - 1:SUBLANES, :]
    hcar[...] = carry
    o_ref[...] = (jnp.concatenate(hs, axis=0) * jax.nn.gelu(ya_ref[...])).astype(BF16)


def _lru(xy, cw, cb, wg_bf, bg, lam, ts=256):
    b, s, _ = xy.shape
    return pl.pallas_call(
        functools.partial(_lru_kernel, ts=ts),
        grid=(b, s // ts),
        in_specs=[
            pl.BlockSpec((None, ts, W_LRU), lambda i, j: (i, j, 0)),
            pl.BlockSpec((None, ts, W_LRU), lambda i, j: (i, j, 1)),
            _resident((CONV_A, W_LRU)),
            _resident((1, W_LRU)),
            _resident((W_LRU, 2 * W_LRU)),
            _resident((1, 2 * W_LRU)),
            _resident((1, W_LRU)),
        ],
        out_specs=pl.BlockSpec((None, ts, W_LRU), lambda i, j: (i, j, 0)),
        out_shape=jax.ShapeDtypeStruct((b, s, W_LRU), BF16),
        scratch_shapes=[
            pltpu.VMEM((SUBLANES + ts, W_LRU), F32),
            pltpu.VMEM((1, W_LRU), F32),
        ],
        compiler_params=_params(("parallel", "arbitrary")),
        name="rglru",
    )(xy, xy, cw, cb, wg_bf, bg, lam)


def _perm_local_key():
    row = lax.broadcasted_iota(jnp.int32, (ATT_BLK, ATT_BLK), 0)
    return (row & (SUBLANES - 1)) * SEG + (row >> 3)


def _sb_block(zneg, run, diag):
    beta = 1.0 / (1.0 + jnp.exp2(zneg))
    om = 1.0 - beta
    if diag:
        earlier = _perm_local_key() < lax.broadcasted_iota(jnp.int32, zneg.shape, 1)
        om = jnp.where(earlier, om, 1.0)
    om_r = [om[r * SUBLANES:(r + 1) * SUBLANES, :] for r in range(SEG)]
    seg_tot = om_r[0]
    for r in range(1, SEG):
        seg_tot = seg_tot * om_r[r]
    sub = lax.broadcasted_iota(jnp.int32, seg_tot.shape, 0)
    tail = jnp.broadcast_to(run, seg_tot.shape)
    for u in range(1, SUBLANES):
        tail = tail * jnp.where(sub < u, seg_tot[u:u + 1, :], 1.0)
    ws = [None] * SEG
    for r in reversed(range(SEG)):
        ws[r] = beta[r * SUBLANES:(r + 1) * SUBLANES, :] * tail
        tail = tail * om_r[r]
    w = jnp.concatenate(ws, axis=0)
    if diag:
        w = jnp.where(earlier, w, 0.0)
    return w.astype(BF16), tail[0:1, :]


def _sb_kernel(qt_ref, k_ref, vt_ref, o_ref):
    i = pl.program_id(1)
    q_all = qt_ref[...].astype(F32)
    head_of_row = lax.broadcasted_iota(jnp.int32, q_all.shape, 0) >> 6
    qs = [jnp.where(head_of_row == h, q_all, 0.0).astype(BF16) for h in range(SB_HEADS)]

    def step(blocks, runs, accs):
        offs = [pl.multiple_of(j * ATT_BLK, ATT_BLK) for j, _, _ in blocks]
        zs = [[jnp.dot(k_ref[pl.ds(off, ATT_BLK), :], qs[h], preferred_element_type=F32)
               for h in range(SB_HEADS)] for off in offs]
        runs, accs = list(runs), list(accs)
        for (_, diag, live), off, z in zip(blocks, offs, zs):
            for h in range(SB_HEADS):
                w, run = _sb_block(z[h], runs[h], diag)
                if live is not None:
                    w = jnp.where(live, w, jnp.zeros_like(w))
                    run = jnp.where(live, run, runs[h])
                runs[h] = run
                vj = vt_ref[h * SB_DH:(h + 1) * SB_DH, pl.ds(off, ATT_BLK)]
                accs[h] = accs[h] + jnp.dot(vj, w, preferred_element_type=F32)
        return tuple(runs), tuple(accs)

    def largest(runs):
        top = runs[0]
        for r in runs[1:]:
            top = jnp.maximum(top, r)
        return jnp.max(top)

    runs = tuple(jnp.ones((1, ATT_BLK), F32) for _ in range(SB_HEADS))
    accs = tuple(jnp.zeros((SB_DH, ATT_BLK), F32) for _ in range(SB_HEADS))
    runs, accs = step([(i, True, None), (jnp.maximum(i - 1, 0), False, i > 0)], runs, accs)

    def cond(c):
        return jnp.logical_and(c[0] >= 0, c[1] > 0.0)

    def body(c):
        j, _, runs, accs = c
        runs, accs = step([(j, False, None)], runs, accs)
        return j - 1, largest(runs), runs, accs

    _, _, _, accs = lax.while_loop(cond, body, (i - 2, largest(runs), runs, accs))
    for h in range(SB_HEADS):
        o_ref[h * SB_DH:(h + 1) * SB_DH, :] = accs[h].astype(o_ref.dtype)


def _stick_breaking(qt, kp, vt):
    b, w, s = qt.shape
    return pl.pallas_call(
        _sb_kernel,
        grid=(b, s // ATT_BLK),
        in_specs=[
            pl.BlockSpec((None, w, ATT_BLK), lambda bi, i: (bi, 0, i)),
            pl.BlockSpec((None, s, w), lambda bi, i: (bi, 0, 0)),
            pl.BlockSpec((None, w, s), lambda bi, i: (bi, 0, 0)),
        ],
        out_specs=pl.BlockSpec((None, w, ATT_BLK), lambda bi, i: (bi, 0, i)),
        out_shape=jax.ShapeDtypeStruct((b, w, s), BF16),
        compiler_params=_params(("parallel", "arbitrary")),
        name="stickbreak",
    )(qt, kp, vt)


DF_SLOPES = tuple(2.0 ** (-8.0 * (h + 1) / DF_HEADS) for h in range(DF_HEADS))
DF_QB = 512
DF_KB = 256
DF_KPQ = DF_QB // DF_KB
DF_SKIP = 150.0
DF_GROUPS = (((0, 1), 1), ((2, 3), 2))


def _df_kernel(q_ref, k_ref, vt_ref, kn_ref, lq1, lk1, lq2, lk2, gs_ref, o_ref, acc_ref, kp_ref,
               *, lam_init):
    i = pl.program_id(1)
    acc_ref[...] = jnp.zeros_like(acc_ref)
    zero = jnp.zeros((DF_DQK, DF_QB), BF16)
    rest = lax.broadcasted_iota(jnp.int32, (DF_KW - 2 * DF_DQK, DF_QB), 0)
    pick_bias = jnp.where(rest < DF_NBIAS, 1.0, 0.0).astype(BF16)
    qa, qb = [], []
    for h in range(DF_HEADS):
        q1 = q_ref[h * DF_DV:h * DF_DV + DF_DQK, :]
        q2 = q_ref[h * DF_DV + DF_DQK:(h + 1) * DF_DV, :]
        qa.append(jnp.concatenate([q1, zero, pick_bias], axis=0))
        qb.append(jnp.concatenate([zero, q2, pick_bias], axis=0))

    def update(s_, m, acc, vj):
        m_new = jnp.maximum(m, jnp.max(s_, axis=0, keepdims=True))
        p = jnp.exp2(s_ - m_new).astype(BF16)
        acc[...] = jnp.exp2(m - m_new) * acc[...] + jnp.dot(vj, p, preferred_element_type=F32)
        return m_new

    def score_dots(offs, heads, in_tile):
        out = []
        for d, off in enumerate(offs):
            lo = d * DF_KB if in_tile else 0
            for h in heads:
                kj = k_ref[pl.ds(off, DF_KB), h * DF_KW:(h + 1) * DF_KW]
                q1, q2 = (qa[h], qb[h]) if lo == 0 else (qa[h][:, lo:], qb[h][:, lo:])
                out.append(jnp.dot(kj, q1, preferred_element_type=F32))
                out.append(jnp.dot(kj, q2, preferred_element_type=F32))
        return out

    def step(tiles, ms, heads, in_tile):
        offs = [pl.multiple_of((jt * DF_KPQ + d) * DF_KB, DF_KB) for jt in tiles for d in range(DF_KPQ)]
        scores = score_dots(offs, heads, in_tile)
        ms = list(ms)
        for d, off in enumerate(offs):
            lo = d * DF_KB if in_tile else 0
            if in_tile:
                kpos = lax.broadcasted_iota(jnp.int32, (DF_KB, DF_QB - lo), 0) + d * DF_KB
                qpos = lax.broadcasted_iota(jnp.int32, (DF_KB, DF_QB - lo), 1) + lo
                ahead = jnp.maximum(kpos - qpos, 0).astype(F32)
                allowed = (kpos >> 6) <= (qpos >> 6)
            for n, h in enumerate(heads):
                vj = vt_ref[h * DF_VR:(h + 1) * DF_VR, pl.ds(off, DF_KB)]
                new = []
                for mp in range(2):
                    s_ = scores[(d * len(heads) + n) * 2 + mp]
                    if in_tile:
                        s_ = jnp.where(allowed, s_ - (2.0 * DF_SLOPES[h] * LOG2E) * ahead, NEG_BIG)
                    m_old = ms[n][mp]
                    if lo == 0:
                        new.append(update(s_, m_old, acc_ref.at[h, mp], vj))
                    else:
                        part = update(s_, m_old[:, lo:], acc_ref.at[h, mp, :, pl.ds(lo, DF_QB - lo)], vj)
                        new.append(jnp.concatenate([m_old[:, :lo], part], axis=1))
                ms[n] = tuple(new)
        return tuple(ms)

    lane = lax.broadcasted_iota(jnp.int32, (1, DF_KW), 1)

    def per_map(values):
        out = jnp.zeros((1, DF_KW), F32)
        for h, pair in values.items():
            for mp, v in enumerate(pair):
                out = jnp.where(lane == 2 * h + mp, v, out)
        return out

    def sq_norm(q):
        sq = q.astype(F32)
        return jnp.max(jnp.sum(sq * sq, axis=0, keepdims=True), axis=1, keepdims=True)

    qn2 = per_map({h: (sq_norm(qa[h][:DF_DQK, :]), sq_norm(qb[h][DF_DQK:2 * DF_DQK, :]))
                   for h in range(DF_HEADS)})
    slope2 = per_map({h: (jnp.full((1, 1), DF_SLOPES[h] * LOG2E, F32),) * 2 for h in range(DF_HEADS)})
    top = kn_ref[0:1, :]
    kp_rows = [top]
    for r in range(1, kn_ref.shape[0]):
        top = jnp.maximum(top, kn_ref[r:r + 1, :])
        kp_rows.append(top)
    kp_ref[...] = jnp.concatenate(kp_rows, axis=0)

    def needed(jt, ms, heads):
        jt = jnp.maximum(jt, 0)
        low = per_map({h: tuple(jnp.min(m, axis=1, keepdims=True) for m in ms[n])
                       for n, h in enumerate(heads)})
        last_key = (jnp.full((1, DF_KW), jt, jnp.int32) * DF_QB + (DF_QB - 1)).astype(F32)
        room = low - (DF_SKIP + 1.0) - slope2 * last_key
        reach = jnp.logical_or(room <= 0.0, qn2 * kp_ref[pl.ds(jt, 1), :] * 1.02 >= room * room)
        mine = jnp.logical_and(lane >= 2 * heads[0], lane < 2 * heads[-1] + 2)
        return jnp.max(jnp.where(jnp.logical_and(mine, reach), 1.0, 0.0))

    m0 = jnp.full((1, DF_QB), NEG_BIG, F32)
    all_heads = tuple(range(DF_HEADS))
    ms_all = step([i], tuple((m0, m0) for _ in all_heads), all_heads, True)
    for heads, span in DF_GROUPS:
        ms = tuple(ms_all[h] for h in heads)
        for r in range(span - 1):
            top = i - 1 - r
            todo = jnp.logical_and(lax.rem(i, span) > r, needed(top, ms, heads) > 0.5)
            ms = lax.cond(todo, lambda ms, top=top, heads=heads: step([top], ms, heads, False),
                          lambda ms: ms, ms)

        def cond(c):
            return jnp.logical_and(c[0] >= 0, c[1] > 0.5)

        def body(c, heads=heads, span=span):
            g, _, ms = c
            ms = step([g * span + r for r in reversed(range(span))], ms, heads, False)
            return g - 1, needed(g * span - 1, ms, heads), ms

        groups = i // span
        lax.while_loop(cond, body, (groups - 1, needed(groups * span - 1, ms, heads), ms))

    lam = (jnp.exp(jnp.sum(lq1[...] * lk1[...], axis=-1, keepdims=True))
           - jnp.exp(jnp.sum(lq2[...] * lk2[...], axis=-1, keepdims=True)) + lam_init)
    for h in range(DF_HEADS):
        a1 = acc_ref[h, 0]
        a2 = acc_ref[h, 1]
        o = a1[:DF_DV, :] / a1[DF_DV:DF_DV + 1, :] - lam * (a2[:DF_DV, :] / a2[DF_DV:DF_DV + 1, :])
        o = o * lax.rsqrt(jnp.mean(o * o, axis=0, keepdims=True) + EPS) * gs_ref[...]
        o_ref[h * DF_DV:(h + 1) * DF_DV, :] = (o * (1.0 - lam_init)).astype(o_ref.dtype)


def _diff_attention(qt, kall, vt, kn, lq1, lk1, lq2, lk2, gs, lam_init):
    b, w, s = qt.shape
    vec = _resident((1, DF_DQK))
    return pl.pallas_call(
        functools.partial(_df_kernel, lam_init=lam_init),
        grid=(b, s // DF_QB),
        in_specs=[
            pl.BlockSpec((None, w, DF_QB), lambda bi, i: (bi, 0, i)),
            pl.BlockSpec((None, s, DF_HEADS * DF_KW), lambda bi, i: (bi, 0, 0)),
            pl.BlockSpec((None, DF_HEADS * DF_VR, s), lambda bi, i: (bi, 0, 0)),
            pl.BlockSpec((None, s // DF_QB, DF_KW), lambda bi, i: (bi, 0, 0)),
            vec, vec, vec, vec,
            _resident((DF_DV, 1)),
        ],
        out_specs=pl.BlockSpec((None, w, DF_QB), lambda bi, i: (bi, 0, i)),
        out_shape=jax.ShapeDtypeStruct((b, w, s), BF16),
        scratch_shapes=[pltpu.VMEM((DF_HEADS, 2, DF_VR, DF_QB), F32),
                        pltpu.VMEM((s // DF_QB, DF_KW), F32)],
        compiler_params=_params(("parallel", "arbitrary")),
        name="diffattn",
    )(qt, kall, vt, kn, lq1, lk1, lq2, lk2, gs)


FF_CHUNK = 256
FF_NCHUNK = D_FF // FF_CHUNK


def _tn_dot(a_t, w):
    return lax.dot_general(a_t, w, (((0,), (0,)), ((), ())), preferred_element_type=F32)


def _ffn_kernel(x_ref, a_ref, bt_ref, ct_ref, wo_ref, g_ref, wu_ref, cw_ref, cb_ref, wd_ref, fg_ref,
                o_ref, ubuf, tails, acc_ref, *, tm, final_norm):
    s = pl.program_id(1)

    @pl.when(s == 0)
    def _():
        tails[...] = jnp.zeros_like(tails)

    mix = jnp.dot(a_ref[...], wo_ref[:W_LRU, :], preferred_element_type=F32)
    mix = mix + _tn_dot(bt_ref[...], wo_ref[W_LRU:W_LRU + SB_W, :])
    mix = mix + _tn_dot(ct_ref[...], wo_ref[W_LRU + SB_W:, :])
    x = x_ref[...] + mix
    h = _rms(x, g_ref[...]).astype(BF16)

    def up(c):
        return tuple(jnp.dot(h, wu_ref[:, pl.ds(col, FF_CHUNK)], preferred_element_type=F32)
                     for col in (c * FF_CHUNK, D_FF + c * FF_CHUNK))

    def conv(c, col, u):
        ubuf[0:SUBLANES, :] = tails[c]
        ubuf[SUBLANES:, :] = u
        tails[c] = u[tm - SUBLANES:, :]
        w = cw_ref[:, pl.ds(col, FF_CHUNK)]
        y = cb_ref[:, pl.ds(col, FF_CHUNK)] + w[CONV_FF - 1:CONV_FF, :] * u
        for k in range(1, CONV_FF):
            y = y + w[CONV_FF - 1 - k:CONV_FF - k, :] * ubuf[pl.ds(SUBLANES - k, tm), :]
        return y

    ups = [up(c) for c in range(FF_NCHUNK)]
    for c in range(FF_NCHUNK):
        ug, uv = ups[c]
        gate = conv(2 * c, c * FF_CHUNK, ug)
        val = conv(2 * c + 1, D_FF + c * FF_CHUNK, uv)
        act = (gate * _sigmoid(gate) * val).astype(BF16)
        part = jnp.dot(act, wd_ref[c * FF_CHUNK:(c + 1) * FF_CHUNK, :], preferred_element_type=F32)
        if c == 0:
            acc_ref[...] = part
        else:
            acc_ref[...] += part
    y = x + acc_ref[...]
    if final_norm:
        y = _rms(y, fg_ref[...])
    o_ref[...] = y


def _ffn(x3, oa, obt, oct, wo_bf, g, wu_bf, cw, cb, wd_bf, fg, final_norm, tm=512):
    b, s, _ = x3.shape
    tok = lambda width: pl.BlockSpec((None, tm, width), lambda i, j: (i, j, 0))
    feat = lambda rows: pl.BlockSpec((None, rows, tm), lambda i, j: (i, 0, j))
    return pl.pallas_call(
        functools.partial(_ffn_kernel, tm=tm, final_norm=final_norm),
        grid=(b, s // tm),
        in_specs=[
            tok(D_MODEL), tok(W_LRU), feat(SB_W), feat(DF_W),
            _resident((MIX, D_MODEL)),
            _resident((1, D_MODEL)),
            _resident((D_MODEL, 2 * D_FF)),
            _resident((CONV_FF, 2 * D_FF)),
            _resident((1, 2 * D_FF)),
            _resident((D_FF, D_MODEL)),
            _resident((1, D_MODEL)),
        ],
        out_specs=pl.BlockSpec((None, tm, D_MODEL), lambda i, j: (i, j, 0)),
        out_shape=jax.ShapeDtypeStruct((b, s, D_MODEL), F32),
        scratch_shapes=[
            pltpu.VMEM((SUBLANES + tm, FF_CHUNK), F32),
            pltpu.VMEM((2 * FF_NCHUNK, SUBLANES, FF_CHUNK), F32),
            pltpu.VMEM((tm, D_MODEL), F32),
        ],
        compiler_params=_params(("parallel", "arbitrary")),
        name="convffn",
    )(x3, oa, obt, oct, wo_bf, g, wu_bf, cw, cb, wd_bf, fg)


def _alibi_bias_table(s):
    resid = (np.asarray(DF_SLOPES, np.float32)[:, None] * np.float32(LOG2E)
             * np.arange(s, dtype=np.float32)[None, :]).astype(np.float32)
    terms = []
    for _ in range(DF_NBIAS):
        top = (resid.view(np.uint32) & np.uint32(0xFFFF0000)).view(np.float32)
        terms.append(top)
        resid = resid - top
    assert not resid.any()
    table = np.zeros((s, DF_HEADS, DF_KW), np.float32)
    table[:, :, 2 * DF_DQK:2 * DF_DQK + DF_NBIAS] = np.stack(terms, axis=-1).transpose(1, 0, 2)
    return jnp.asarray(table.reshape(s, DF_HEADS * DF_KW)).astype(BF16)


def _block_diag(w):
    n, bw, _ = w.shape
    eye = jnp.eye(n, dtype=w.dtype)
    return jnp.einsum("ncd,nm->ncmd", w, eye).reshape(n * bw, n * bw)


def kernel(x, norm1_g, w_in, conv_a_w, conv_a_b, w_rgate, b_rgate, w_igate, b_igate, lru_lambda, lam_q1, lam_k1, lam_q2, lam_k2, subln_g, w_out, norm2_g, w_ff_up, conv_ff_w, conv_ff_b, w_ff_down, final_g):
    s = x.shape[1]
    bias_tab = _alibi_bias_table(s)
    for l in range(DEPTH):
        lam_init = 0.8 - 0.6 * float(np.exp(-0.3 * l))
        xy, dk, skp, sqt, dqt, dvt, svt, kn = _inproj(x, norm1_g[l][None, :], w_in[l], bias_tab)

        wg = jnp.concatenate([_block_diag(w_rgate[l]), _block_diag(w_igate[l])], axis=1).astype(BF16)
        bg = jnp.concatenate([b_rgate[l], b_igate[l]])[None, :]
        out_a = _lru(xy, conv_a_w[l], conv_a_b[l][None, :], wg, bg, lru_lambda[l][None, :])
        out_bt = _stick_breaking(sqt, skp, svt)
        out_ct = _diff_attention(dqt, dk, dvt, kn[:, :, 0, :], lam_q1[l][None, :], lam_k1[l][None, :],
                                 lam_q2[l][None, :], lam_k2[l][None, :], subln_g[l][:, None], lam_init)

        x = _ffn(x, out_a, out_bt, out_ct, w_out[l].astype(BF16), norm2_g[l][None, :],
                 w_ff_up[l].astype(BF16), conv_ff_w[l], conv_ff_b[l][None, :],
                 w_ff_down[l].astype(BF16), final_g[None, :], l == DEPTH - 1)
    return x
```

```python
import functools

import jax
import jax.numpy as jnp
import numpy as np
from jax import lax
from jax.experimental import pallas as pl
from jax.experimental.pallas import tpu as pltpu

D_MODEL = 1024
BATCH = 4
SEQ = 8192
DEPTH = 2
CHUNK = 64
MIX = D_MODEL
W_LRU = MIX // 2
LRU_BLOCKS = 8
LRU_BW = W_LRU // LRU_BLOCKS
LRU_C = 8.0
CONV_A = 4
SB_HEADS = 4
SB_DH = MIX // 4 // SB_HEADS
SB_W = SB_HEADS * SB_DH
DF_HEADS = 4
DF_DV = MIX // 4 // DF_HEADS
DF_DQK = DF_DV // 2
DF_W = DF_HEADS * DF_DV
D_FF = ((8 * D_MODEL // 3 + 255) // 256) * 256
CONV_FF = 3
EPS = 1e-6
P_IN = 2 * W_LRU + 3 * SB_W + 3 * DF_W

SUBLANES = 8
VMEM_LIMIT = 56 * 1024 * 1024

ATT_BLK = 256
SEG = ATT_BLK // SUBLANES
NEG_BIG = -1e30
LOG2E = 1.4426950408889634

F32 = jnp.float32
BF16 = jnp.bfloat16


def _params(sem, vmem=VMEM_LIMIT):
    return pltpu.CompilerParams(dimension_semantics=sem, vmem_limit_bytes=vmem)


def _resident(shape):
    nd = len(shape)
    return pl.BlockSpec(shape, lambda *_: (0,) * nd, pipeline_mode=pl.Buffered(1))


def _rms(x, g):
    return x * lax.rsqrt(jnp.mean(x * x, axis=-1, keepdims=True) + EPS) * g


DF_KW = 128
DF_NBIAS = 3
DF_VR = 80


def _nt_dot(w_t, h):
    return lax.dot_general(w_t, h, (((1,), (1,)), ((), ())), preferred_element_type=F32)


def _inproj_kernel(x_ref, g_ref, w_tok_ref, w_sk_ref, wt_ref, wt_sv_ref, bias_ref, ones_ref, perm_ref,
                   sel_ref, xy_ref, dk_ref, skp_ref, sqt_ref, dqt_ref, dvt_ref, svt_ref, kn_ref, *, tm):
    h = _rms(x_ref[...], g_ref[...]).astype(BF16)
    perm = perm_ref[...]
    xy = jnp.dot(h, w_tok_ref[:, :2 * W_LRU], preferred_element_type=F32)
    dk = jnp.dot(h, w_tok_ref[:, 2 * W_LRU:], preferred_element_type=F32)
    sk = jnp.dot(h, w_sk_ref[...], preferred_element_type=F32)
    svt = _nt_dot(wt_sv_ref[...], h)
    sqt = _nt_dot(wt_ref[:SB_W, :], h)
    dqt = _nt_dot(wt_ref[SB_W:SB_W + DF_W, :], h)
    dvt = _nt_dot(wt_ref[SB_W + DF_W:, :], h)
    xy_ref[...] = xy
    dk = dk.astype(BF16)
    dk_ref[...] = dk + bias_ref[...]
    dkf = dk.astype(F32)
    norms = jnp.dot((dkf * dkf).astype(BF16), sel_ref[...], preferred_element_type=F32)
    kn_ref[...] = jnp.broadcast_to(jnp.max(norms, axis=0, keepdims=True), kn_ref.shape)
    sk = sk.astype(BF16)
    svt = svt.astype(BF16)
    for blk in range(tm // ATT_BLK):
        lo, hi = blk * ATT_BLK, (blk + 1) * ATT_BLK
        skp_ref[lo:hi, :] = jnp.dot(perm, sk[lo:hi, :], preferred_element_type=F32).astype(BF16)
        svt_ref[:, lo:hi] = _nt_dot(svt[:, lo:hi], perm).astype(BF16)
    sqt_ref[...] = (sqt * (-(SB_DH ** -0.5) * LOG2E)).astype(BF16)
    dqt_ref[...] = (dqt * (DF_DQK ** -0.5 * LOG2E)).astype(BF16)
    dvt_ref[...] = (dvt + ones_ref[...]).astype(BF16)


def _inproj(x, g, w, bias_tab, tm=512):
    b, s, d = x.shape
    o = 2 * W_LRU
    w_sq, w_sk, w_sv = (w[:, o + k * SB_W:o + (k + 1) * SB_W] for k in range(3))
    o += 3 * SB_W
    w_dq, w_dk, w_dv = (w[:, o + k * DF_W:o + (k + 1) * DF_W] for k in range(3))
    dk_pad = jnp.pad(w_dk.reshape(d, DF_HEADS, DF_DV), ((0, 0), (0, 0), (0, DF_KW - DF_DV)))
    dv_pad = jnp.pad(w_dv.T.reshape(DF_HEADS, DF_DV, d), ((0, 0), (0, DF_VR - DF_DV), (0, 0)))
    w_tok = jnp.concatenate([w[:, :2 * W_LRU], dk_pad.reshape(d, DF_HEADS * DF_KW)], axis=1).astype(BF16)
    wt = jnp.concatenate([w_sq.T, w_dq.T, dv_pad.reshape(DF_HEADS * DF_VR, d)], axis=0).astype(BF16)
    ones_col = np.zeros((DF_HEADS * DF_VR, 1), np.float32)
    ones_col[DF_DV::DF_VR] = 1.0
    new_row = np.arange(ATT_BLK)
    perm = np.zeros((ATT_BLK, ATT_BLK), np.float32)
    perm[new_row, (new_row % SUBLANES) * SEG + new_row // SUBLANES] = 1.0
    sel = np.zeros((DF_HEADS, DF_KW, DF_KW), np.float32)
    for h in range(DF_HEADS):
        for mp in range(2):
            sel[h, mp * DF_DQK:(mp + 1) * DF_DQK, 2 * h + mp] = 1.0
    sel = sel.reshape(DF_HEADS * DF_KW, DF_KW)
    assert tm == DF_QB
    tok = lambda width: pl.BlockSpec((None, tm, width), lambda i, j: (i, j, 0))
    feat = lambda rows: pl.BlockSpec((None, rows, tm), lambda i, j: (i, 0, j))
    return pl.pallas_call(
        functools.partial(_inproj_kernel, tm=tm),
        grid=(b, s // tm),
        in_specs=[
            tok(d),
            _resident((1, d)),
            _resident(w_tok.shape),
            _resident((d, SB_W)),
            _resident(wt.shape),
            _resident((SB_W, d)),
            pl.BlockSpec((tm, DF_HEADS * DF_KW), lambda i, j: (j, 0)),
            _resident(ones_col.shape),
            _resident(perm.shape),
            _resident(sel.shape),
        ],
        out_specs=[tok(2 * W_LRU), tok(DF_HEADS * DF_KW), tok(SB_W),
                   feat(SB_W), feat(DF_W), feat(DF_HEADS * DF_VR), feat(SB_W),
                   pl.BlockSpec((None, None, SUBLANES, DF_KW), lambda i, j: (i, j, 0, 0))],
        out_shape=[
            jax.ShapeDtypeStruct((b, s, 2 * W_LRU), F32),
            jax.ShapeDtypeStruct((b, s, DF_HEADS * DF_KW), BF16),
            jax.ShapeDtypeStruct((b, s, SB_W), BF16),
            jax.ShapeDtypeStruct((b, SB_W, s), BF16),
            jax.ShapeDtypeStruct((b, DF_W, s), BF16),
            jax.ShapeDtypeStruct((b, DF_HEADS * DF_VR, s), BF16),
            jax.ShapeDtypeStruct((b, SB_W, s), BF16),
            jax.ShapeDtypeStruct((b, s // tm, SUBLANES, DF_KW), F32),
        ],
        compiler_params=_params(("parallel", "parallel")),
        name="inproj",
    )(x, g, w_tok, w_sk.astype(BF16), wt, w_sv.T.astype(BF16), bias_tab, jnp.asarray(ones_col),
      jnp.asarray(perm).astype(BF16), jnp.asarray(sel).astype(BF16))


def _softplus(x):
    return jnp.maximum(x, 0.0) + jnp.log(1.0 + jnp.exp(-jnp.abs(x)))


def _sigmoid(x):
    return 1.0 / (1.0 + jnp.exp(-x))


def _lru_kernel(xa_ref, ya_ref, cw_ref, cb_ref, wg_ref, bg_ref, lam_ref, o_ref,
                xbuf, hcar, *, ts):
    s = pl.program_id(1)

    @pl.when(s == 0)
    def _():
        xbuf[0:SUBLANES, :] = jnp.zeros((SUBLANES, W_LRU), F32)
        hcar[...] = jnp.zeros_like(hcar)

    xbuf[SUBLANES:, :] = xa_ref[...]
    xc = cb_ref[...] + cw_ref[CONV_A - 1:CONV_A, :] * xa_ref[...]
    for k in range(1, CONV_A):
        xc = xc + cw_ref[CONV_A - 1 - k:CONV_A - k, :] * xbuf[pl.ds(SUBLANES - k, ts), :]
    xbuf[0:SUBLANES, :] = xa_ref[ts - SUBLANES:, :]

    gates = jnp.dot(xc.astype(BF16), wg_ref[...], preferred_element_type=F32) + bg_ref[...]
    r = _sigmoid(gates[:, :W_LRU])
    ig = _sigmoid(gates[:, W_LRU:])
    log_a = (-LRU_C) * r * _softplus(-lam_ref[...])
    a = jnp.exp(log_a)
    u = jnp.sqrt(1.0 - a * a) * (ig * xc)

    groups = ts // SUBLANES
    a = a.reshape(groups, SUBLANES, W_LRU)
    u = u.reshape(groups, SUBLANES, W_LRU)
    sub = lax.broadcasted_iota(jnp.int32, a.shape, 1)
    d = 1
    while d < SUBLANES:
        a_sh = jnp.where(sub >= d, pltpu.roll(a, d, axis=1), 1.0)
        u_sh = jnp.where(sub >= d, pltpu.roll(u, d, axis=1), 0.0)
        u = a * u_sh + u
        a = a * a_sh
        d *= 2
    carry = hcar[...]
    hs = []
    for grp in range(groups):
        hs.append(a[grp] * carry + u[grp])
        carry = hs[-1][SUBLANES - 1:SUBLANES, :]
    hcar[...] = carry
    o_ref[...] = (jnp.concatenate(hs, axis=0) * jax.nn.gelu(ya_ref[...])).astype(BF16)


def _lru(xy, cw, cb, wg_bf, bg, lam, ts=256):
    b, s, _ = xy.shape
    return pl.pallas_call(
        functools.partial(_lru_kernel, ts=ts),
        grid=(b, s // ts),
        in_specs=[
            pl.BlockSpec((None, ts, W_LRU), lambda i, j: (i, j, 0)),
            pl.BlockSpec((None, ts, W_LRU), lambda i, j: (i, j, 1)),
            _resident((CONV_A, W_LRU)),
            _resident((1, W_LRU)),
            _resident((W_LRU, 2 * W_LRU)),
            _resident((1, 2 * W_LRU)),
            _resident((1, W_LRU)),
        ],
        out_specs=pl.BlockSpec((None, ts, W_LRU), lambda i, j: (i, j, 0)),
        out_shape=jax.ShapeDtypeStruct((b, s, W_LRU), BF16),
        scratch_shapes=[
            pltpu.VMEM((SUBLANES + ts, W_LRU), F32),
            pltpu.VMEM((1, W_LRU), F32),
        ],
        compiler_params=_params(("parallel", "arbitrary")),
        name="rglru",
    )(xy, xy, cw, cb, wg_bf, bg, lam)


def _perm_local_key():
    row = lax.broadcasted_iota(jnp.int32, (ATT_BLK, ATT_BLK), 0)
    return (row & (SUBLANES - 1)) * SEG + (row >> 3)


def _sb_block(zneg, run, diag):
    beta = 1.0 / (1.0 + jnp.exp2(zneg))
    om = 1.0 - beta
    if diag:
        earlier = _perm_local_key() < lax.broadcasted_iota(jnp.int32, zneg.shape, 1)
        om = jnp.where(earlier, om, 1.0)
    om_r = [om[r * SUBLANES:(r + 1) * SUBLANES, :] for r in range(SEG)]
    seg_tot = om_r[0]
    for r in range(1, SEG):
        seg_tot = seg_tot * om_r[r]
    sub = lax.broadcasted_iota(jnp.int32, seg_tot.shape, 0)
    tail = jnp.broadcast_to(run, seg_tot.shape)
    for u in range(1, SUBLANES):
        tail = tail * jnp.where(sub < u, seg_tot[u:u + 1, :], 1.0)
    ws = [None] * SEG
    for r in reversed(range(SEG)):
        ws[r] = beta[r * SUBLANES:(r + 1) * SUBLANES, :] * tail
        tail = tail * om_r[r]
    w = jnp.concatenate(ws, axis=0)
    if diag:
        w = jnp.where(earlier, w, 0.0)
    return w.astype(BF16), tail[0:1, :]


def _sb_kernel(qt_ref, k_ref, vt_ref, o_ref):
    i = pl.program_id(1)
    q_all = qt_ref[...].astype(F32)
    head_of_row = lax.broadcasted_iota(jnp.int32, q_all.shape, 0) >> 6
    qs = [jnp.where(head_of_row == h, q_all, 0.0).astype(BF16) for h in range(SB_HEADS)]

    def step(blocks, runs, accs):
        offs = [pl.multiple_of(j * ATT_BLK, ATT_BLK) for j, _, _ in blocks]
        zs = [[jnp.dot(k_ref[pl.ds(off, ATT_BLK), :], qs[h], preferred_element_type=F32)
               for h in range(SB_HEADS)] for off in offs]
        runs, accs = list(runs), list(accs)
        for (_, diag, live), off, z in zip(blocks, offs, zs):
            for h in range(SB_HEADS):
                w, run = _sb_block(z[h], runs[h], diag)
                if live is not None:
                    w = jnp.where(live, w, jnp.zeros_like(w))
                    run = jnp.where(live, run, runs[h])
                runs[h] = run
                vj = vt_ref[h * SB_DH:(h + 1) * SB_DH, pl.ds(off, ATT_BLK)]
                accs[h] = accs[h] + jnp.dot(vj, w, preferred_element_type=F32)
        return tuple(runs), tuple(accs)

    def largest(runs):
        top = runs[0]
        for r in runs[1:]:
            top = jnp.maximum(top, r)
        return jnp.max(top)

    runs = tuple(jnp.ones((1, ATT_BLK), F32) for _ in range(SB_HEADS))
    accs = tuple(jnp.zeros((SB_DH, ATT_BLK), F32) for _ in range(SB_HEADS))
    runs, accs = step([(i, True, None), (jnp.maximum(i - 1, 0), False, i > 0)], runs, accs)

    def cond(c):
        return jnp.logical_and(c[0] >= 0, c[1] > 0.0)

    def body(c):
        j, _, runs, accs = c
        runs, accs = step([(j, False, None)], runs, accs)
        return j - 1, largest(runs), runs, accs

    _, _, _, accs = lax.while_loop(cond, body, (i - 2, largest(runs), runs, accs))
    for h in range(SB_HEADS):
        o_ref[h * SB_DH:(h + 1) * SB_DH, :] = accs[h].astype(o_ref.dtype)


def _stick_breaking(qt, kp, vt):
    b, w, s = qt.shape
    return pl.pallas_call(
        _sb_kernel,
        grid=(b, s // ATT_BLK),
        in_specs=[
            pl.BlockSpec((None, w, ATT_BLK), lambda bi, i: (bi, 0, i)),
            pl.BlockSpec((None, s, w), lambda bi, i: (bi, 0, 0)),
            pl.BlockSpec((None, w, s), lambda bi, i: (bi, 0, 0)),
        ],
        out_specs=pl.BlockSpec((None, w, ATT_BLK), lambda bi, i: (bi, 0, i)),
        out_shape=jax.ShapeDtypeStruct((b, w, s), BF16),
        compiler_params=_params(("parallel", "arbitrary")),
        name="stickbreak",
    )(qt, kp, vt)


DF_SLOPES = tuple(2.0 ** (-8.0 * (h + 1) / DF_HEADS) for h in range(DF_HEADS))
DF_QB = 512
DF_KB = 256
DF_KPQ = DF_QB // DF_KB
DF_SKIP = 150.0
DF_GROUPS = (((0, 1), 1), ((2, 3), 2))


def _df_kernel(q_ref, k_ref, vt_ref, kn_ref, lq1, lk1, lq2, lk2, gs_ref, o_ref, acc_ref, kp_ref,
               *, lam_init):
    i = pl.program_id(1)
    acc_ref[...] = jnp.zeros_like(acc_ref)
    zero = jnp.zeros((DF_DQK, DF_QB), BF16)
    rest = lax.broadcasted_iota(jnp.int32, (DF_KW - 2 * DF_DQK, DF_QB), 0)
    pick_bias = jnp.where(rest < DF_NBIAS, 1.0, 0.0).astype(BF16)
    qa, qb = [], []
    for h in range(DF_HEADS):
        q1 = q_ref[h * DF_DV:h * DF_DV + DF_DQK, :]
        q2 = q_ref[h * DF_DV + DF_DQK:(h + 1) * DF_DV, :]
        qa.append(jnp.concatenate([q1, zero, pick_bias], axis=0))
        qb.append(jnp.concatenate([zero, q2, pick_bias], axis=0))

    def update(s_, m, acc, vj):
        m_new = jnp.maximum(m, jnp.max(s_, axis=0, keepdims=True))
        p = jnp.exp2(s_ - m_new).astype(BF16)
        acc[...] = jnp.exp2(m - m_new) * acc[...] + jnp.dot(vj, p, preferred_element_type=F32)
        return m_new

    def score_dots(offs, heads, in_tile):
        out = []
        for d, off in enumerate(offs):
            lo = d * DF_KB if in_tile else 0
            for h in heads:
                kj = k_ref[pl.ds(off, DF_KB), h * DF_KW:(h + 1) * DF_KW]
                q1, q2 = (qa[h], qb[h]) if lo == 0 else (qa[h][:, lo:], qb[h][:, lo:])
                out.append(jnp.dot(kj, q1, preferred_element_type=F32))
                out.append(jnp.dot(kj, q2, preferred_element_type=F32))
        return out

    def step(tiles, ms, heads, in_tile):
        offs = [pl.multiple_of((jt * DF_KPQ + d) * DF_KB, DF_KB) for jt in tiles for d in range(DF_KPQ)]
        scores = score_dots(offs, heads, in_tile)
        ms = list(ms)
        for d, off in enumerate(offs):
            lo = d * DF_KB if in_tile else 0
            if in_tile:
                kpos = lax.broadcasted_iota(jnp.int32, (DF_KB, DF_QB - lo), 0) + d * DF_KB
                qpos = lax.broadcasted_iota(jnp.int32, (DF_KB, DF_QB - lo), 1) + lo
                ahead = jnp.maximum(kpos - qpos, 0).astype(F32)
                allowed = (kpos >> 6) <= (qpos >> 6)
            for n, h in enumerate(heads):
                vj = vt_ref[h * DF_VR:(h + 1) * DF_VR, pl.ds(off, DF_KB)]
                new = []
                for mp in range(2):
                    s_ = scores[(d * len(heads) + n) * 2 + mp]
                    if in_tile:
                        s_ = jnp.where(allowed, s_ - (2.0 * DF_SLOPES[h] * LOG2E) * ahead, NEG_BIG)
                    m_old = ms[n][mp]
                    if lo == 0:
                        new.append(update(s_, m_old, acc_ref.at[h, mp], vj))
                    else:
                        part = update(s_, m_old[:, lo:], acc_ref.at[h, mp, :, pl.ds(lo, DF_QB - lo)], vj)
                        new.append(jnp.concatenate([m_old[:, :lo], part], axis=1))
                ms[n] = tuple(new)
        return tuple(ms)

    lane = lax.broadcasted_iota(jnp.int32, (1, DF_KW), 1)

    def per_map(values):
        out = jnp.zeros((1, DF_KW), F32)
        for h, pair in values.items():
            for mp, v in enumerate(pair):
                out = jnp.where(lane == 2 * h + mp, v, out)
        return out

    def sq_norm(q):
        sq = q.astype(F32)
        return jnp.max(jnp.sum(sq * sq, axis=0, keepdims=True), axis=1, keepdims=True)

    qn2 = per_map({h: (sq_norm(qa[h][:DF_DQK, :]), sq_norm(qb[h][DF_DQK:2 * DF_DQK, :]))
                   for h in range(DF_HEADS)})
    slope2 = per_map({h: (jnp.full((1, 1), DF_SLOPES[h] * LOG2E, F32),) * 2 for h in range(DF_HEADS)})
    top = kn_ref[0:1, :]
    kp_rows = [top]
    for r in range(1, kn_ref.shape[0]):
        top = jnp.maximum(top, kn_ref[r:r + 1, :])
        kp_rows.append(top)
    kp_ref[...] = jnp.concatenate(kp_rows, axis=0)

    def needed(jt, ms, heads):
        jt = jnp.maximum(jt, 0)
        low = per_map({h: tuple(jnp.min(m, axis=1, keepdims=True) for m in ms[n])
                       for n, h in enumerate(heads)})
        last_key = (jnp.full((1, DF_KW), jt, jnp.int32) * DF_QB + (DF_QB - 1)).astype(F32)
        room = low - (DF_SKIP + 1.0) - slope2 * last_key
        reach = jnp.logical_or(room <= 0.0, qn2 * kp_ref[pl.ds(jt, 1), :] * 1.02 >= room * room)
        mine = jnp.logical_and(lane >= 2 * heads[0], lane < 2 * heads[-1] + 2)
        return jnp.max(jnp.where(jnp.logical_and(mine, reach), 1.0, 0.0))

    m0 = jnp.full((1, DF_QB), NEG_BIG, F32)
    all_heads = tuple(range(DF_HEADS))
    ms_all = step([i], tuple((m0, m0) for _ in all_heads), all_heads, True)
    for heads, span in DF_GROUPS:
        ms = tuple(ms_all[h] for h in heads)
        for r in range(span - 1):
            top = i - 1 - r
            todo = jnp.logical_and(lax.rem(i, span) > r, needed(top, ms, heads) > 0.5)
            ms = lax.cond(todo, lambda ms, top=top, heads=heads: step([top], ms, heads, False),
                          lambda ms: ms, ms)

        def cond(c):
            return jnp.logical_and(c[0] >= 0, c[1] > 0.5)

        def body(c, heads=heads, span=span):
            g, _, ms = c
            ms = step([g * span + r for r in reversed(range(span))], ms, heads, False)
            return g - 1, needed(g * span - 1, ms, heads), ms

        groups = i // span
        lax.while_loop(cond, body, (groups - 1, needed(groups * span - 1, ms, heads), ms))

    lam = (jnp.exp(jnp.sum(lq1[...] * lk1[...], axis=-1, keepdims=True))
           - jnp.exp(jnp.sum(lq2[...] * lk2[...], axis=-1, keepdims=True)) + lam_init)
    for h in range(DF_HEADS):
        a1 = acc_ref[h, 0]
        a2 = acc_ref[h, 1]
        o = a1[:DF_DV, :] / a1[DF_DV:DF_DV + 1, :] - lam * (a2[:DF_DV, :] / a2[DF_DV:DF_DV + 1, :])
        o = o * lax.rsqrt(jnp.mean(o * o, axis=0, keepdims=True) + EPS) * gs_ref[...]
        o_ref[h * DF_DV:(h + 1) * DF_DV, :] = (o * (1.0 - lam_init)).astype(o_ref.dtype)


def _diff_attention(qt, kall, vt, kn, lq1, lk1, lq2, lk2, gs, lam_init):
    b, w, s = qt.shape
    vec = _resident((1, DF_DQK))
    return pl.pallas_call(
        functools.partial(_df_kernel, lam_init=lam_init),
        grid=(b, s // DF_QB),
        in_specs=[
            pl.BlockSpec((None, w, DF_QB), lambda bi, i: (bi, 0, i)),
            pl.BlockSpec((None, s, DF_HEADS * DF_KW), lambda bi, i: (bi, 0, 0)),
            pl.BlockSpec((None, DF_HEADS * DF_VR, s), lambda bi, i: (bi, 0, 0)),
            pl.BlockSpec((None, s // DF_QB, DF_KW), lambda bi, i: (bi, 0, 0)),
            vec, vec, vec, vec,
            _resident((DF_DV, 1)),
        ],
        out_specs=pl.BlockSpec((None, w, DF_QB), lambda bi, i: (bi, 0, i)),
        out_shape=jax.ShapeDtypeStruct((b, w, s), BF16),
        scratch_shapes=[pltpu.VMEM((DF_HEADS, 2, DF_VR, DF_QB), F32),
                        pltpu.VMEM((s // DF_QB, DF_KW), F32)],
        compiler_params=_params(("parallel", "arbitrary")),
        name="diffattn",
    )(qt, kall, vt, kn, lq1, lk1, lq2, lk2, gs)


FF_CHUNK = 256
FF_NCHUNK = D_FF // FF_CHUNK


def _tn_dot(a_t, w):
    return lax.dot_general(a_t, w, (((0,), (0,)), ((), ())), preferred_element_type=F32)


def _ffn_kernel(x_ref, a_ref, bt_ref, ct_ref, wo_ref, g_ref, wu_ref, cw_ref, cb_ref, wd_ref, fg_ref,
                o_ref, ubuf, tails, acc_ref, *, tm, final_norm):
    s = pl.program_id(1)

    @pl.when(s == 0)
    def _():
        tails[...] = jnp.zeros_like(tails)

    mix = jnp.dot(a_ref[...], wo_ref[:W_LRU, :], preferred_element_type=F32)
    mix = mix + _tn_dot(bt_ref[...], wo_ref[W_LRU:W_LRU + SB_W, :])
    mix = mix + _tn_dot(ct_ref[...], wo_ref[W_LRU + SB_W:, :])
    x = x_ref[...] + mix
    h = _rms(x, g_ref[...]).astype(BF16)

    def up(c):
        return tuple(jnp.dot(h, wu_ref[:, pl.ds(col, FF_CHUNK)], preferred_element_type=F32)
                     for col in (c * FF_CHUNK, D_FF + c * FF_CHUNK))

    def conv(c, col, u):
        ubuf[0:SUBLANES, :] = tails[c]
        ubuf[SUBLANES:, :] = u
        tails[c] = u[tm - SUBLANES:, :]
        w = cw_ref[:, pl.ds(col, FF_CHUNK)]
        y = cb_ref[:, pl.ds(col, FF_CHUNK)] + w[CONV_FF - 1:CONV_FF, :] * u
        for k in range(1, CONV_FF):
            y = y + w[CONV_FF - 1 - k:CONV_FF - k, :] * ubuf[pl.ds(SUBLANES - k, tm), :]
        return y

    ups = [up(c) for c in range(FF_NCHUNK)]
    for c in range(FF_NCHUNK):
        ug, uv = ups[c]
        gate = conv(2 * c, c * FF_CHUNK, ug)
        val = conv(2 * c + 1, D_FF + c * FF_CHUNK, uv)
        act = (gate * _sigmoid(gate) * val).astype(BF16)
        part = jnp.dot(act, wd_ref[c * FF_CHUNK:(c + 1) * FF_CHUNK, :], preferred_element_type=F32)
        if c == 0:
            acc_ref[...] = part
        else:
            acc_ref[...] += part
    y = x + acc_ref[...]
    if final_norm:
        y = _rms(y, fg_ref[...])
    o_ref[...] = y


def _ffn(x3, oa, obt, oct, wo_bf, g, wu_bf, cw, cb, wd_bf, fg, final_norm, tm=512):
    b, s, _ = x3.shape
    tok = lambda width: pl.BlockSpec((None, tm, width), lambda i, j: (i, j, 0))
    feat = lambda rows: pl.BlockSpec((None, rows, tm), lambda i, j: (i, 0, j))
    return pl.pallas_call(
        functools.partial(_ffn_kernel, tm=tm, final_norm=final_norm),
        grid=(b, s // tm),
        in_specs=[
            tok(D_MODEL), tok(W_LRU), feat(SB_W), feat(DF_W),
            _resident((MIX, D_MODEL)),
            _resident((1, D_MODEL)),
            _resident((D_MODEL, 2 * D_FF)),
            _resident((CONV_FF, 2 * D_FF)),
            _resident((1, 2 * D_FF)),
            _resident((D_FF, D_MODEL)),
            _resident((1, D_MODEL)),
        ],
        out_specs=pl.BlockSpec((None, tm, D_MODEL), lambda i, j: (i, j, 0)),
        out_shape=jax.ShapeDtypeStruct((b, s, D_MODEL), F32),
        scratch_shapes=[
            pltpu.VMEM((SUBLANES + tm, FF_CHUNK), F32),
            pltpu.VMEM((2 * FF_NCHUNK, SUBLANES, FF_CHUNK), F32),
            pltpu.VMEM((tm, D_MODEL), F32),
        ],
        compiler_params=_params(("parallel", "arbitrary")),
        name="convffn",
    )(x3, oa, obt, oct, wo_bf, g, wu_bf, cw, cb, wd_bf, fg)


def _alibi_bias_table(s):
    resid = (np.asarray(DF_SLOPES, np.float32)[:, None] * np.float32(LOG2E)
             * np.arange(s, dtype=np.float32)[None, :]).astype(np.float32)
    terms = []
    for _ in range(DF_NBIAS):
        top = (resid.view(np.uint32) & np.uint32(0xFFFF0000)).view(np.float32)
        terms.append(top)
        resid = resid - top
    assert not resid.any()
    table = np.zeros((s, DF_HEADS, DF_KW), np.float32)
    table[:, :, 2 * DF_DQK:2 * DF_DQK + DF_NBIAS] = np.stack(terms, axis=-1).transpose(1, 0, 2)
    return jnp.asarray(table.reshape(s, DF_HEADS * DF_KW)).astype(BF16)


def _block_diag(w):
    n, bw, _ = w.shape
    eye = jnp.eye(n, dtype=w.dtype)
    return jnp.einsum("ncd,nm->ncmd", w, eye).reshape(n * bw, n * bw)


def kernel(x, norm1_g, w_in, conv_a_w, conv_a_b, w_rgate, b_rgate, w_igate, b_igate, lru_lambda, lam_q1, lam_k1, lam_q2, lam_k2, subln_g, w_out, norm2_g, w_ff_up, conv_ff_w, conv_ff_b, w_ff_down, final_g):
    s = x.shape[1]
    bias_tab = _alibi_bias_table(s)
    for l in range(DEPTH):
        lam_init = 0.8 - 0.6 * float(np.exp(-0.3 * l))
        xy, dk, skp, sqt, dqt, dvt, svt, kn = _inproj(x, norm1_g[l][None, :], w_in[l], bias_tab)

        wg = jnp.concatenate([_block_diag(w_rgate[l]), _block_diag(w_igate[l])], axis=1).astype(BF16)
        bg = jnp.concatenate([b_rgate[l], b_igate[l]])[None, :]
        out_a = _lru(xy, conv_a_w[l], conv_a_b[l][None, :], wg, bg, lru_lambda[l][None, :])
        out_bt = _stick_breaking(sqt, skp, svt)
        out_ct = _diff_attention(dqt, dk, dvt, kn[:, :, 0, :], lam_q1[l][None, :], lam_k1[l][None, :],
                                 lam_q2[l][None, :], lam_k2[l][None, :], subln_g[l][:, None], lam_init)

        x = _ffn(x, out_a, out_bt, out_ct, w_out[l].astype(BF16), norm2_g[l][None, :],
                 w_ff_up[l].astype(BF16), conv_ff_w[l], conv_ff_b[l][None, :],
                 w_ff_down[l].astype(BF16), final_g[None, :], l == DEPTH - 1)
    return x
```

```python
import functools

import jax
import jax.numpy as jnp
import numpy as np
from jax import lax
from jax.experimental import pallas as pl
from jax.experimental.pallas import tpu as pltpu

D_MODEL = 1024
BATCH = 4
SEQ = 8192
DEPTH = 2
CHUNK = 64
MIX = D_MODEL
W_LRU = MIX // 2
LRU_BLOCKS = 8
LRU_BW = W_LRU // LRU_BLOCKS
LRU_C = 8.0
CONV_A = 4
SB_HEADS = 4
SB_DH = MIX // 4 // SB_HEADS
SB_W = SB_HEADS * SB_DH
DF_HEADS = 4
DF_DV = MIX // 4 // DF_HEADS
DF_DQK = DF_DV // 2
DF_W = DF_HEADS * DF_DV
D_FF = ((8 * D_MODEL // 3 + 255) // 256) * 256
CONV_FF = 3
EPS = 1e-6
P_IN = 2 * W_LRU + 3 * SB_W + 3 * DF_W

SUBLANES = 8
VMEM_LIMIT = 56 * 1024 * 1024

ATT_BLK = 256
SEG = ATT_BLK // SUBLANES
NEG_BIG = -1e30
LOG2E = 1.4426950408889634

F32 = jnp.float32
BF16 = jnp.bfloat16


def _params(sem, vmem=VMEM_LIMIT):
    return pltpu.CompilerParams(dimension_semantics=sem, vmem_limit_bytes=vmem)


def _resident(shape):
    nd = len(shape)
    return pl.BlockSpec(shape, lambda *_: (0,) * nd, pipeline_mode=pl.Buffered(1))


def _rms(x, g):
    return x * lax.rsqrt(jnp.mean(x * x, axis=-1, keepdims=True) + EPS) * g


DF_KW = 128
DF_NBIAS = 3
DF_VR = 80


def _nt_dot(w_t, h):
    return lax.dot_general(w_t, h, (((1,), (1,)), ((), ())), preferred_element_type=F32)


def _inproj_kernel(x_ref, g_ref, w_tok_ref, w_sk_ref, wt_ref, wt_sv_ref, bias_ref, ones_ref, perm_ref,
                   sel_ref, xy_ref, dk_ref, skp_ref, sqt_ref, dqt_ref, dvt_ref, svt_ref, kn_ref, *, tm):
    h = _rms(x_ref[...], g_ref[...]).astype(BF16)
    perm = perm_ref[...]
    xy = jnp.dot(h, w_tok_ref[:, :2 * W_LRU], preferred_element_type=F32)
    dk = jnp.dot(h, w_tok_ref[:, 2 * W_LRU:], preferred_element_type=F32)
    sk = jnp.dot(h, w_sk_ref[...], preferred_element_type=F32)
    svt = _nt_dot(wt_sv_ref[...], h)
    sqt = _nt_dot(wt_ref[:SB_W, :], h)
    dqt = _nt_dot(wt_ref[SB_W:SB_W + DF_W, :], h)
    dvt = _nt_dot(wt_ref[SB_W + DF_W:, :], h)
    xy_ref[...] = xy
    dk = dk.astype(BF16)
    dk_ref[...] = dk + bias_ref[...]
    dkf = dk.astype(F32)
    norms = jnp.dot((dkf * dkf).astype(BF16), sel_ref[...], preferred_element_type=F32)
    kn_ref[...] = jnp.broadcast_to(jnp.max(norms, axis=0, keepdims=True), kn_ref.shape)
    sk = sk.astype(BF16)
    svt = svt.astype(BF16)
    for blk in range(tm // ATT_BLK):
        lo, hi = blk * ATT_BLK, (blk + 1) * ATT_BLK
        skp_ref[lo:hi, :] = jnp.dot(perm, sk[lo:hi, :], preferred_element_type=F32).astype(BF16)
        svt_ref[:, lo:hi] = _nt_dot(svt[:, lo:hi], perm).astype(BF16)
    sqt_ref[...] = (sqt * (-(SB_DH ** -0.5) * LOG2E)).astype(BF16)
    dqt_ref[...] = (dqt * (DF_DQK ** -0.5 * LOG2E)).astype(BF16)
    dvt_ref[...] = (dvt + ones_ref[...]).astype(BF16)


def _inproj(x, g, w, bias_tab, tm=512):
    b, s, d = x.shape
    o = 2 * W_LRU
    w_sq, w_sk, w_sv = (w[:, o + k * SB_W:o + (k + 1) * SB_W] for k in range(3))
    o += 3 * SB_W
    w_dq, w_dk, w_dv = (w[:, o + k * DF_W:o + (k + 1) * DF_W] for k in range(3))
    dk_pad = jnp.pad(w_dk.reshape(d, DF_HEADS, DF_DV), ((0, 0), (0, 0), (0, DF_KW - DF_DV)))
    dv_pad = jnp.pad(w_dv.T.reshape(DF_HEADS, DF_DV, d), ((0, 0), (0, DF_VR - DF_DV), (0, 0)))
    w_tok = jnp.concatenate([w[:, :2 * W_LRU], dk_pad.reshape(d, DF_HEADS * DF_KW)], axis=1).astype(BF16)
    wt = jnp.concatenate([w_sq.T, w_dq.T, dv_pad.reshape(DF_HEADS * DF_VR, d)], axis=0).astype(BF16)
    ones_col = np.zeros((DF_HEADS * DF_VR, 1), np.float32)
    ones_col[DF_DV::DF_VR] = 1.0
    new_row = np.arange(ATT_BLK)
    perm = np.zeros((ATT_BLK, ATT_BLK), np.float32)
    perm[new_row, (new_row % SUBLANES) * SEG + new_row // SUBLANES] = 1.0
    sel = np.zeros((DF_HEADS, DF_KW, DF_KW), np.float32)
    for h in range(DF_HEADS):
        for mp in range(2):
            sel[h, mp * DF_DQK:(mp + 1) * DF_DQK, 2 * h + mp] = 1.0
    sel = sel.reshape(DF_HEADS * DF_KW, DF_KW)
    assert tm == DF_QB
    tok = lambda width: pl.BlockSpec((None, tm, width), lambda i, j: (i, j, 0))
    feat = lambda rows: pl.BlockSpec((None, rows, tm), lambda i, j: (i, 0, j))
    return pl.pallas_call(
        functools.partial(_inproj_kernel, tm=tm),
        grid=(b, s // tm),
        in_specs=[
            tok(d),
            _resident((1, d)),
            _resident(w_tok.shape),
            _resident((d, SB_W)),
            _resident(wt.shape),
            _resident((SB_W, d)),
            pl.BlockSpec((tm, DF_HEADS * DF_KW), lambda i, j: (j, 0)),
            _resident(ones_col.shape),
            _resident(perm.shape),
            _resident(sel.shape),
        ],
        out_specs=[tok(2 * W_LRU), tok(DF_HEADS * DF_KW), tok(SB_W),
                   feat(SB_W), feat(DF_W), feat(DF_HEADS * DF_VR), feat(SB_W),
                   pl.BlockSpec((None, None, SUBLANES, DF_KW), lambda i, j: (i, j, 0, 0))],
        out_shape=[
            jax.ShapeDtypeStruct((b, s, 2 * W_LRU), F32),
            jax.ShapeDtypeStruct((b, s, DF_HEADS * DF_KW), BF16),
            jax.ShapeDtypeStruct((b, s, SB_W), BF16),
            jax.ShapeDtypeStruct((b, SB_W, s), BF16),
            jax.ShapeDtypeStruct((b, DF_W, s), BF16),
            jax.ShapeDtypeStruct((b, DF_HEADS * DF_VR, s), BF16),
            jax.ShapeDtypeStruct((b, SB_W, s), BF16),
            jax.ShapeDtypeStruct((b, s // tm, SUBLANES, DF_KW), F32),
        ],
        compiler_params=_params(("parallel", "parallel")),
        name="inproj",
    )(x, g, w_tok, w_sk.astype(BF16), wt, w_sv.T.astype(BF16), bias_tab, jnp.asarray(ones_col),
      jnp.asarray(perm).astype(BF16), jnp.asarray(sel).astype(BF16))


def _softplus(x):
    return jnp.maximum(x, 0.0) + jnp.log(1.0 + jnp.exp(-jnp.abs(x)))


def _sigmoid(x):
    return 1.0 / (1.0 + jnp.exp(-x))


def _lru_kernel(xa_ref, ya_ref, cw_ref, cb_ref, wg_ref, bg_ref, lam_ref, o_ref,
                xbuf, hcar, *, ts):
    s = pl.program_id(1)

    @pl.when(s == 0)
    def _():
        xbuf[0:SUBLANES, :] = jnp.zeros((SUBLANES, W_LRU), F32)
        hcar[...] = jnp.zeros_like(hcar)

    xbuf[SUBLANES:, :] = xa_ref[...]
    xc = cb_ref[...] + cw_ref[CONV_A - 1:CONV_A, :] * xa_ref[...]
    for k in range(1, CONV_A):
        xc = xc + cw_ref[CONV_A - 1 - k:CONV_A - k, :] * xbuf[pl.ds(SUBLANES - k, ts), :]
    xbuf[0:SUBLANES, :] = xa_ref[ts - SUBLANES:, :]

    gates = jnp.dot(xc.astype(BF16), wg_ref[...], preferred_element_type=F32) + bg_ref[...]
    r = _sigmoid(gates[:, :W_LRU])
    ig = _sigmoid(gates[:, W_LRU:])
    log_a = (-LRU_C) * r * _softplus(-lam_ref[...])
    a = jnp.exp(log_a)
    u = jnp.sqrt(1.0 - a * a) * (ig * xc)

    groups = ts // SUBLANES
    a = a.reshape(groups, SUBLANES, W_LRU)
    u = u.reshape(groups, SUBLANES, W_LRU)
    sub = lax.broadcasted_iota(jnp.int32, a.shape, 1)
    d = 1
    while d < SUBLANES:
        a_sh = jnp.where(sub >= d, pltpu.roll(a, d, axis=1), 1.0)
        u_sh = jnp.where(sub >= d, pltpu.roll(u, d, axis=1), 0.0)
        u = a * u_sh + u
        a = a * a_sh
        d *= 2
    carry = hcar[...]
    hs = []
    for grp in range(groups):
        hs.append(a[grp] * carry + u[grp])
        carry = hs[-1][SUBLANES - 1:SUBLANES, :]
    hcar[...] = carry
    o_ref[...] = (jnp.concatenate(hs, axis=0) * jax.nn.gelu(ya_ref[...])).astype(BF16)


def _lru(xy, cw, cb, wg_bf, bg, lam, ts=256):
    b, s, _ = xy.shape
    return pl.pallas_call(
        functools.partial(_lru_kernel, ts=ts),
        grid=(b, s // ts),
        in_specs=[
            pl.BlockSpec((None, ts, W_LRU), lambda i, j: (i, j, 0)),
            pl.BlockSpec((None, ts, W_LRU), lambda i, j: (i, j, 1)),
            _resident((CONV_A, W_LRU)),
            _resident((1, W_LRU)),
            _resident((W_LRU, 2 * W_LRU)),
            _resident((1, 2 * W_LRU)),
            _resident((1, W_LRU)),
        ],
        out_specs=pl.BlockSpec((None, ts, W_LRU), lambda i, j: (i, j, 0)),
        out_shape=jax.ShapeDtypeStruct((b, s, W_LRU), BF16),
        scratch_shapes=[
            pltpu.VMEM((SUBLANES + ts, W_LRU), F32),
            pltpu.VMEM((1, W_LRU), F32),
        ],
        compiler_params=_params(("parallel", "arbitrary")),
        name="rglru",
    )(xy, xy, cw, cb, wg_bf, bg, lam)


def _perm_local_key():
    row = lax.broadcasted_iota(jnp.int32, (ATT_BLK, ATT_BLK), 0)
    return (row & (SUBLANES - 1)) * SEG + (row >> 3)


def _sb_block(zneg, run, diag):
    beta = 1.0 / (1.0 + jnp.exp2(zneg))
    om = 1.0 - beta
    if diag:
        earlier = _perm_local_key() < lax.broadcasted_iota(jnp.int32, zneg.shape, 1)
        om = jnp.where(earlier, om, 1.0)
    om_r = [om[r * SUBLANES:(r + 1) * SUBLANES, :] for r in range(SEG)]
    seg_tot = om_r[0]
    for r in range(1, SEG):
        seg_tot = seg_tot * om_r[r]
    sub = lax.broadcasted_iota(jnp.int32, seg_tot.shape, 0)
    tail = jnp.broadcast_to(run, seg_tot.shape)
    for u in range(1, SUBLANES):
        tail = tail * jnp.where(sub < u, seg_tot[u:u + 1, :], 1.0)
    ws = [None] * SEG
    for r in reversed(range(SEG)):
        ws[r] = beta[r * SUBLANES:(r + 1) * SUBLANES, :] * tail
        tail = tail * om_r[r]
    w = jnp.concatenate(ws, axis=0)
    if diag:
        w = jnp.where(earlier, w, 0.0)
    return w.astype(BF16), tail[0:1, :]


def _sb_kernel(qt_ref, k_ref, vt_ref, o_ref):
    i = pl.program_id(1)
    q_all = qt_ref[...].astype(F32)
    head_of_row = lax.broadcasted_iota(jnp.int32, q_all.shape, 0) >> 6
    qs = [jnp.where(head_of_row == h, q_all, 0.0).astype(BF16) for h in range(SB_HEADS)]

    def step(blocks, runs, accs):
        offs = [pl.multiple_of(j * ATT_BLK, ATT_BLK) for j, _, _ in blocks]
        zs = [[jnp.dot(k_ref[pl.ds(off, ATT_BLK), :], qs[h], preferred_element_type=F32)
               for h in range(SB_HEADS)] for off in offs]
        runs, accs = list(runs), list(accs)
        for (_, diag, live), off, z in zip(blocks, offs, zs):
            for h in range(SB_HEADS):
                w, run = _sb_block(z[h], runs[h], diag)
                if live is not None:
                    w = jnp.where(live, w, jnp.zeros_like(w))
                    run = jnp.where(live, run, runs[h])
                runs[h] = run
                vj = vt_ref[h * SB_DH:(h + 1) * SB_DH, pl.ds(off, ATT_BLK)]
                accs[h] = accs[h] + jnp.dot(vj, w, preferred_element_type=F32)
        return tuple(runs), tuple(accs)

    def largest(runs):
        top = runs[0]
        for r in runs[1:]:
            top = jnp.maximum(top, r)
        return jnp.max(top)

    runs = tuple(jnp.ones((1, ATT_BLK), F32) for _ in range(SB_HEADS))
    accs = tuple(jnp.zeros((SB_DH, ATT_BLK), F32) for _ in range(SB_HEADS))
    runs, accs = step([(i, True, None), (jnp.maximum(i - 1, 0), False, i > 0)], runs, accs)

    def cond(c):
        return jnp.logical_and(c[0] >= 0, c[1] > 0.0)

    def body(c):
        j, _, runs, accs = c
        runs, accs = step([(j, False, None)], runs, accs)
        return j - 1, largest(runs), runs, accs

    _, _, _, accs = lax.while_loop(cond, body, (i - 2, largest(runs), runs, accs))
    for h in range(SB_HEADS):
        o_ref[h * SB_DH:(h + 1) * SB_DH, :] = accs[h].astype(o_ref.dtype)


def _stick_breaking(qt, kp, vt):
    b, w, s = qt.shape
    return pl.pallas_call(
        _sb_kernel,
        grid=(b, s // ATT_BLK),
        in_specs=[
            pl.BlockSpec((None, w, ATT_BLK), lambda bi, i: (bi, 0, i)),
            pl.BlockSpec((None, s, w), lambda bi, i: (bi, 0, 0)),
            pl.BlockSpec((None, w, s), lambda bi, i: (bi, 0, 0)),
        ],
        out_specs=pl.BlockSpec((None, w, ATT_BLK), lambda bi, i: (bi, 0, i)),
        out_shape=jax.ShapeDtypeStruct((b, w, s), BF16),
        compiler_params=_params(("parallel", "arbitrary")),
        name="stickbreak",
    )(qt, kp, vt)


DF_SLOPES = tuple(2.0 ** (-8.0 * (h + 1) / DF_HEADS) for h in range(DF_HEADS))
DF_QB = 512
DF_KB = 256
DF_KPQ = DF_QB // DF_KB
DF_SKIP = 150.0
DF_GROUPS = (((0, 1), 1), ((2, 3), 2))


def _df_kernel(q_ref, k_ref, vt_ref, kn_ref, lq1, lk1, lq2, lk2, gs_ref, o_ref, acc_ref, kp_ref,
               *, lam_init):
    i = pl.program_id(1)
    acc_ref[...] = jnp.zeros_like(acc_ref)
    zero = jnp.zeros((DF_DQK, DF_QB), BF16)
    rest = lax.broadcasted_iota(jnp.int32, (DF_KW - 2 * DF_DQK, DF_QB), 0)
    pick_bias = jnp.where(rest < DF_NBIAS, 1.0, 0.0).astype(BF16)
    qa, qb = [], []
    for h in range(DF_HEADS):
        q1 = q_ref[h * DF_DV:h * DF_DV + DF_DQK, :]
        q2 = q_ref[h * DF_DV + DF_DQK:(h + 1) * DF_DV, :]
        qa.append(jnp.concatenate([q1, zero, pick_bias], axis=0))
        qb.append(jnp.concatenate([zero, q2, pick_bias], axis=0))

    def update(s_, m, acc, vj):
        m_new = jnp.maximum(m, jnp.max(s_, axis=0, keepdims=True))
        p = jnp.exp2(s_ - m_new).astype(BF16)
        acc[...] = jnp.exp2(m - m_new) * acc[...] + jnp.dot(vj, p, preferred_element_type=F32)
        return m_new

    def score_dots(offs, heads, in_tile):
        out = []
        for d, off in enumerate(offs):
            lo = d * DF_KB if in_tile else 0
            for h in heads:
                kj = k_ref[pl.ds(off, DF_KB), h * DF_KW:(h + 1) * DF_KW]
                q1, q2 = (qa[h], qb[h]) if lo == 0 else (qa[h][:, lo:], qb[h][:, lo:])
                out.append(jnp.dot(kj, q1, preferred_element_type=F32))
                out.append(jnp.dot(kj, q2, preferred_element_type=F32))
        return out

    def step(tiles, ms, heads, in_tile):
        offs = [pl.multiple_of((jt * DF_KPQ + d) * DF_KB, DF_KB) for jt in tiles for d in range(DF_KPQ)]
        scores = score_dots(offs, heads, in_tile)
        if not in_tile:
            ms = [list(pair) for pair in ms]
            for n, h in enumerate(heads):
                for mp in range(2):
                    for d, off in enumerate(offs):
                        vj = vt_ref[h * DF_VR:(h + 1) * DF_VR, pl.ds(off, DF_KB)]
                        ms[n][mp] = update(scores[(d * len(heads) + n) * 2 + mp], ms[n][mp],
                                           acc_ref.at[h, mp], vj)
            return tuple(tuple(pair) for pair in ms)
        ms = list(ms)
        for d, off in enumerate(offs):
            lo = d * DF_KB if in_tile else 0
            if in_tile:
                kpos = lax.broadcasted_iota(jnp.int32, (DF_KB, DF_QB - lo), 0) + d * DF_KB
                qpos = lax.broadcasted_iota(jnp.int32, (DF_KB, DF_QB - lo), 1) + lo
                ahead = jnp.maximum(kpos - qpos, 0).astype(F32)
                allowed = (kpos >> 6) <= (qpos >> 6)
            for n, h in enumerate(heads):
                vj = vt_ref[h * DF_VR:(h + 1) * DF_VR, pl.ds(off, DF_KB)]
                new = []
                for mp in range(2):
                    s_ = scores[(d * len(heads) + n) * 2 + mp]
                    if in_tile:
                        s_ = jnp.where(allowed, s_ - (2.0 * DF_SLOPES[h] * LOG2E) * ahead, NEG_BIG)
                    m_old = ms[n][mp]
                    if lo == 0:
                        new.append(update(s_, m_old, acc_ref.at[h, mp], vj))
                    else:
                        part = update(s_, m_old[:, lo:], acc_ref.at[h, mp, :, pl.ds(lo, DF_QB - lo)], vj)
                        new.append(jnp.concatenate([m_old[:, :lo], part], axis=1))
                ms[n] = tuple(new)
        return tuple(ms)

    lane = lax.broadcasted_iota(jnp.int32, (1, DF_KW), 1)

    def per_map(values):
        out = jnp.zeros((1, DF_KW), F32)
        for h, pair in values.items():
            for mp, v in enumerate(pair):
                out = jnp.where(lane == 2 * h + mp, v, out)
        return out

    def sq_norm(q):
        sq = q.astype(F32)
        return jnp.max(jnp.sum(sq * sq, axis=0, keepdims=True), axis=1, keepdims=True)

    qn2 = per_map({h: (sq_norm(qa[h][:DF_DQK, :]), sq_norm(qb[h][DF_DQK:2 * DF_DQK, :]))
                   for h in range(DF_HEADS)})
    slope2 = per_map({h: (jnp.full((1, 1), DF_SLOPES[h] * LOG2E, F32),) * 2 for h in range(DF_HEADS)})
    top = kn_ref[0:1, :]
    kp_rows = [top]
    for r in range(1, kn_ref.shape[0]):
        top = jnp.maximum(top, kn_ref[r:r + 1, :])
        kp_rows.append(top)
    kp_ref[...] = jnp.concatenate(kp_rows, axis=0)

    def needed(jt, ms, heads):
        jt = jnp.maximum(jt, 0)
        low = per_map({h: tuple(jnp.min(m, axis=1, keepdims=True) for m in ms[n])
                       for n, h in enumerate(heads)})
        last_key = (jnp.full((1, DF_KW), jt, jnp.int32) * DF_QB + (DF_QB - 1)).astype(F32)
        room = low - (DF_SKIP + 1.0) - slope2 * last_key
        reach = jnp.logical_or(room <= 0.0, qn2 * kp_ref[pl.ds(jt, 1), :] * 1.02 >= room * room)
        mine = jnp.logical_and(lane >= 2 * heads[0], lane < 2 * heads[-1] + 2)
        return jnp.max(jnp.where(jnp.logical_and(mine, reach), 1.0, 0.0))

    m0 = jnp.full((1, DF_QB), NEG_BIG, F32)
    all_heads = tuple(range(DF_HEADS))
    ms_all = step([i], tuple((m0, m0) for _ in all_heads), all_heads, True)
    for heads, span in DF_GROUPS:
        ms = tuple(ms_all[h] for h in heads)
        for r in range(span - 1):
            top = i - 1 - r
            todo = jnp.logical_and(lax.rem(i, span) > r, needed(top, ms, heads) > 0.5)
            ms = lax.cond(todo, lambda ms, top=top, heads=heads: step([top], ms, heads, False),
                          lambda ms: ms, ms)

        def cond(c):
            return jnp.logical_and(c[0] >= 0, c[1] > 0.5)

        def body(c, heads=heads, span=span):
            g, _, ms = c
            ms = step([g * span + r for r in reversed(range(span))], ms, heads, False)
            return g - 1, needed(g * span - 1, ms, heads), ms

        groups = i // span
        lax.while_loop(cond, body, (groups - 1, needed(groups * span - 1, ms, heads), ms))

    lam = (jnp.exp(jnp.sum(lq1[...] * lk1[...], axis=-1, keepdims=True))
           - jnp.exp(jnp.sum(lq2[...] * lk2[...], axis=-1, keepdims=True)) + lam_init)
    for h in range(DF_HEADS):
        a1 = acc_ref[h, 0]
        a2 = acc_ref[h, 1]
        o = a1[:DF_DV, :] / a1[DF_DV:DF_DV + 1, :] - lam * (a2[:DF_DV, :] / a2[DF_DV:DF_DV + 1, :])
        o = o * lax.rsqrt(jnp.mean(o * o, axis=0, keepdims=True) + EPS) * gs_ref[...]
        o_ref[h * DF_DV:(h + 1) * DF_DV, :] = (o * (1.0 - lam_init)).astype(o_ref.dtype)


def _diff_attention(qt, kall, vt, kn, lq1, lk1, lq2, lk2, gs, lam_init):
    b, w, s = qt.shape
    vec = _resident((1, DF_DQK))
    return pl.pallas_call(
        functools.partial(_df_kernel, lam_init=lam_init),
        grid=(b, s // DF_QB),
        in_specs=[
            pl.BlockSpec((None, w, DF_QB), lambda bi, i: (bi, 0, i)),
            pl.BlockSpec((None, s, DF_HEADS * DF_KW), lambda bi, i: (bi, 0, 0)),
            pl.BlockSpec((None, DF_HEADS * DF_VR, s), lambda bi, i: (bi, 0, 0)),
            pl.BlockSpec((None, s // DF_QB, DF_KW), lambda bi, i: (bi, 0, 0)),
            vec, vec, vec, vec,
            _resident((DF_DV, 1)),
        ],
        out_specs=pl.BlockSpec((None, w, DF_QB), lambda bi, i: (bi, 0, i)),
        out_shape=jax.ShapeDtypeStruct((b, w, s), BF16),
        scratch_shapes=[pltpu.VMEM((DF_HEADS, 2, DF_VR, DF_QB), F32),
                        pltpu.VMEM((s // DF_QB, DF_KW), F32)],
        compiler_params=_params(("parallel", "arbitrary")),
        name="diffattn",
    )(qt, kall, vt, kn, lq1, lk1, lq2, lk2, gs)


FF_CHUNK = 256
FF_NCHUNK = D_FF // FF_CHUNK


def _tn_dot(a_t, w):
    return lax.dot_general(a_t, w, (((0,), (0,)), ((), ())), preferred_element_type=F32)


def _ffn_kernel(x_ref, a_ref, bt_ref, ct_ref, wo_ref, g_ref, wu_ref, cw_ref, cb_ref, wd_ref, fg_ref,
                o_ref, ubuf, tails, acc_ref, *, tm, final_norm):
    s = pl.program_id(1)

    @pl.when(s == 0)
    def _():
        tails[...] = jnp.zeros_like(tails)

    mix = jnp.dot(a_ref[...], wo_ref[:W_LRU, :], preferred_element_type=F32)
    mix = mix + _tn_dot(bt_ref[...], wo_ref[W_LRU:W_LRU + SB_W, :])
    mix = mix + _tn_dot(ct_ref[...], wo_ref[W_LRU + SB_W:, :])
    x = x_ref[...] + mix
    h = _rms(x, g_ref[...]).astype(BF16)

    def up(c):
        return tuple(jnp.dot(h, wu_ref[:, pl.ds(col, FF_CHUNK)], preferred_element_type=F32)
                     for col in (c * FF_CHUNK, D_FF + c * FF_CHUNK))

    def conv(c, col, u):
        ubuf[0:SUBLANES, :] = tails[c]
        ubuf[SUBLANES:, :] = u
        tails[c] = u[tm - SUBLANES:, :]
        w = cw_ref[:, pl.ds(col, FF_CHUNK)]
        y = cb_ref[:, pl.ds(col, FF_CHUNK)] + w[CONV_FF - 1:CONV_FF, :] * u
        for k in range(1, CONV_FF):
            y = y + w[CONV_FF - 1 - k:CONV_FF - k, :] * ubuf[pl.ds(SUBLANES - k, tm), :]
        return y

    ups = [up(c) for c in range(FF_NCHUNK)]
    for c in range(FF_NCHUNK):
        ug, uv = ups[c]
        gate = conv(2 * c, c * FF_CHUNK, ug)
        val = conv(2 * c + 1, D_FF + c * FF_CHUNK, uv)
        act = (gate * _sigmoid(gate) * val).astype(BF16)
        part = jnp.dot(act, wd_ref[c * FF_CHUNK:(c + 1) * FF_CHUNK, :], preferred_element_type=F32)
        if c == 0:
            acc_ref[...] = part
        else:
            acc_ref[...] += part
    y = x + acc_ref[...]
    if final_norm:
        y = _rms(y, fg_ref[...])
    o_ref[...] = y


def _ffn(x3, oa, obt, oct, wo_bf, g, wu_bf, cw, cb, wd_bf, fg, final_norm, tm=512):
    b, s, _ = x3.shape
    tok = lambda width: pl.BlockSpec((None, tm, width), lambda i, j: (i, j, 0))
    feat = lambda rows: pl.BlockSpec((None, rows, tm), lambda i, j: (i, 0, j))
    return pl.pallas_call(
        functools.partial(_ffn_kernel, tm=tm, final_norm=final_norm),
        grid=(b, s // tm),
        in_specs=[
            tok(D_MODEL), tok(W_LRU), feat(SB_W), feat(DF_W),
            _resident((MIX, D_MODEL)),
            _resident((1, D_MODEL)),
            _resident((D_MODEL, 2 * D_FF)),
            _resident((CONV_FF, 2 * D_FF)),
            _resident((1, 2 * D_FF)),
            _resident((D_FF, D_MODEL)),
            _resident((1, D_MODEL)),
        ],
        out_specs=pl.BlockSpec((None, tm, D_MODEL), lambda i, j: (i, j, 0)),
        out_shape=jax.ShapeDtypeStruct((b, s, D_MODEL), F32),
        scratch_shapes=[
            pltpu.VMEM((SUBLANES + tm, FF_CHUNK), F32),
            pltpu.VMEM((2 * FF_NCHUNK, SUBLANES, FF_CHUNK), F32),
            pltpu.VMEM((tm, D_MODEL), F32),
        ],
        compiler_params=_params(("parallel", "arbitrary")),
        name="convffn",
    )(x3, oa, obt, oct, wo_bf, g, wu_bf, cw, cb, wd_bf, fg)


def _alibi_bias_table(s):
    resid = (np.asarray(DF_SLOPES, np.float32)[:, None] * np.float32(LOG2E)
             * np.arange(s, dtype=np.float32)[None, :]).astype(np.float32)
    terms = []
    for _ in range(DF_NBIAS):
        top = (resid.view(np.uint32) & np.uint32(0xFFFF0000)).view(np.float32)
        terms.append(top)
        resid = resid - top
    assert not resid.any()
    table = np.zeros((s, DF_HEADS, DF_KW), np.float32)
    table[:, :, 2 * DF_DQK:2 * DF_DQK + DF_NBIAS] = np.stack(terms, axis=-1).transpose(1, 0, 2)
    return jnp.asarray(table.reshape(s, DF_HEADS * DF_KW)).astype(BF16)


def _block_diag(w):
    n, bw, _ = w.shape
    eye = jnp.eye(n, dtype=w.dtype)
    return jnp.einsum("ncd,nm->ncmd", w, eye).reshape(n * bw, n * bw)


def kernel(x, norm1_g, w_in, conv_a_w, conv_a_b, w_rgate, b_rgate, w_igate, b_igate, lru_lambda, lam_q1, lam_k1, lam_q2, lam_k2, subln_g, w_out, norm2_g, w_ff_up, conv_ff_w, conv_ff_b, w_ff_down, final_g):
    s = x.shape[1]
    bias_tab = _alibi_bias_table(s)
    for l in range(DEPTH):
        lam_init = 0.8 - 0.6 * float(np.exp(-0.3 * l))
        xy, dk, skp, sqt, dqt, dvt, svt, kn = _inproj(x, norm1_g[l][None, :], w_in[l], bias_tab)

        wg = jnp.concatenate([_block_diag(w_rgate[l]), _block_diag(w_igate[l])], axis=1).astype(BF16)
        bg = jnp.concatenate([b_rgate[l], b_igate[l]])[None, :]
        out_a = _lru(xy, conv_a_w[l], conv_a_b[l][None, :], wg, bg, lru_lambda[l][None, :])
        out_bt = _stick_breaking(sqt, skp, svt)
        out_ct = _diff_attention(dqt, dk, dvt, kn[:, :, 0, :], lam_q1[l][None, :], lam_k1[l][None, :],
                                 lam_q2[l][None, :], lam_k2[l][None, :], subln_g[l][:, None], lam_init)

        x = _ffn(x, out_a, out_bt, out_ct, w_out[l].astype(BF16), norm2_g[l][None, :],
                 w_ff_up[l].astype(BF16), conv_ff_w[l], conv_ff_b[l][None, :],
                 w_ff_down[l].astype(BF16), final_g[None, :], l == DEPTH - 1)
    return x
```
